```python
import jax, jax.numpy as jnp
from jax import lax
import numpy as np

D_MODEL = 1024
BATCH = 8
SEQ = 2048
DEPTH = 1

N_ATTN_HEADS = 8
HEAD_DIM = 64
D_ATTN = N_ATTN_HEADS * HEAD_DIM
D_POOL = D_MODEL - D_ATTN
POOL_WINDOWS = (2, 4, 8, 16)
N_POOL_GROUPS = len(POOL_WINDOWS)
POOL_GROUP_DIM = D_POOL // N_POOL_GROUPS
D_MIX = D_ATTN + D_POOL
D_IN_PROJ = 3 * D_ATTN + D_POOL
MOBA_BLOCK = 256
MOBA_TOPK = 3
Q_CHUNK = 128
ROPE_THETA = 10000.0
D_FF = ((8 * D_MODEL // 3 + 127) // 128) * 128
CONV_WIDTH = 3
RMS_EPS = 1e-6
NEG_INF = -1e30

kernel_name = "hybrid_moba_pool_convffn_layer"


def rmsnorm(x, g):
    xf = x.astype(jnp.float32)
    y = xf * lax.rsqrt(jnp.mean(xf * xf, axis=-1, keepdims=True) + RMS_EPS)
    return (y * g.astype(jnp.float32)).astype(x.dtype)


def rope(x):
    S, Dh = x.shape[1], x.shape[-1]
    inv_freq = ROPE_THETA ** (-jnp.arange(0, Dh, 2, dtype=jnp.float32) / Dh)
    ang = jnp.arange(S, dtype=jnp.float32)[:, None] * inv_freq[None, :]
    cos, sin = jnp.cos(ang)[:, None, :], jnp.sin(ang)[:, None, :]
    xf = x.astype(jnp.float32)
    x1, x2 = xf[..., : Dh // 2], xf[..., Dh // 2:]
    return jnp.concatenate([x1 * cos - x2 * sin, x2 * cos + x1 * sin], axis=-1).astype(x.dtype)


def moba_attention(q, k, v):
    B, H, S, Dh = q.shape
    nb = -(-S // MOBA_BLOCK)
    pad = nb * MOBA_BLOCK - S
    kb = jnp.pad(k, ((0, 0), (0, 0), (0, pad), (0, 0))).reshape(B, H, nb, MOBA_BLOCK, Dh)
    vb = jnp.pad(v, ((0, 0), (0, 0), (0, pad), (0, 0))).reshape(B, H, nb, MOBA_BLOCK, Dh)
    scale = HEAD_DIM ** -0.5
    n_qc = S // Q_CHUNK
    n_sel = min(MOBA_TOPK, nb - 1)

    def to_chunks(a):
        rest = a.shape[3:]
        a = a.reshape((B, H, n_qc, Q_CHUNK) + rest)
        a = jnp.moveaxis(a, 2, 1)
        return a.reshape((B * n_qc, H, Q_CHUNK) + rest)

    qc = to_chunks(q)
    step_idx = jnp.arange(B * n_qc)
    bidx, cidx = step_idx // n_qc, step_idx % n_qc
    h_idx = jnp.arange(H)[:, None, None]

    def own_block_scores(q_c, kb_b, ci):
        ob = (ci * Q_CHUNK) // MOBA_BLOCK
        qpos = ci * Q_CHUNK + jnp.arange(Q_CHUNK)
        kpos = ob * MOBA_BLOCK + jnp.arange(MOBA_BLOCK)
        k_own = kb_b[:, ob]
        s = jnp.einsum('hqd,hkd->hqk', q_c, k_own).astype(jnp.float32) * scale
        return jnp.where(kpos[None, None, :] <= qpos[None, :, None], s, NEG_INF), ob

    if n_sel > 0:
        k_mean = jnp.mean(kb.astype(jnp.float32), axis=3)
        gate = jnp.einsum('bhsd,bhnd->bhsn', q.astype(jnp.float32), k_mean)
        q_blk = jnp.arange(S) // MOBA_BLOCK
        past = jnp.arange(nb)[None, :] < q_blk[:, None]
        gate = jnp.where(past, gate, -jnp.inf)
        _, sel = lax.top_k(gate, n_sel)
        sel_ok = sel < q_blk[:, None]
        selc, okc = to_chunks(sel), to_chunks(sel_ok)

        def step(args):
            bi, ci, q_c, sel_c, ok_c = args
            kb_b, vb_b = kb[bi], vb[bi]
            s_own, ob = own_block_scores(q_c, kb_b, ci)
            k_sel = kb_b[h_idx, sel_c]
            v_sel = vb_b[h_idx, sel_c]
            s_sel = jnp.einsum('hqd,hqnkd->hqnk', q_c, k_sel).astype(jnp.float32) * scale
            s_sel = jnp.where(ok_c[..., None], s_sel, NEG_INF).reshape(H, Q_CHUNK, n_sel * MOBA_BLOCK)
            p = jax.nn.softmax(jnp.concatenate([s_sel, s_own], axis=-1), axis=-1).astype(v.dtype)
            p_sel = p[..., : n_sel * MOBA_BLOCK].reshape(H, Q_CHUNK, n_sel, MOBA_BLOCK)
            p_own = p[..., n_sel * MOBA_BLOCK:]
            return (jnp.einsum('hqnk,hqnkd->hqd', p_sel, v_sel)
                    + jnp.einsum('hqk,hkd->hqd', p_own, vb_b[:, ob]))

        out = lax.map(step, (bidx, cidx, qc, selc, okc))
    else:
        def step(args):
            bi, ci, q_c = args
            kb_b, vb_b = kb[bi], vb[bi]
            s_own, ob = own_block_scores(q_c, kb_b, ci)
            p = jax.nn.softmax(s_own, axis=-1).astype(v.dtype)
            return jnp.einsum('hqk,hkd->hqd', p, vb_b[:, ob])

        out = lax.map(step, (bidx, cidx, qc))
    out = out.reshape(B, n_qc, H, Q_CHUNK, Dh).transpose(0, 1, 3, 2, 4)
    return out.reshape(B, S, H * Dh)


def pool_mixer(u, w_pool, b_pool, pool_scale):
    B, S, _ = u.shape
    ug = u.reshape(B, S, N_POOL_GROUPS, POOL_GROUP_DIM)
    c = jnp.pad(jnp.cumsum(ug.astype(jnp.float32), axis=1), ((0, 0), (1, 0), (0, 0), (0, 0)))
    t = jnp.arange(S)
    means = []
    for g, w in enumerate(POOL_WINDOWS):
        lo = jnp.maximum(t + 1 - w, 0)
        cnt = (t + 1 - lo).astype(jnp.float32)
        means.append((c[:, t + 1, g] - c[:, lo, g]) / cnt[None, :, None])
    diff = (jnp.stack(means, axis=2) - ug.astype(jnp.float32)).astype(u.dtype)
    y = jnp.einsum('bsgi,gio->bsgo', diff, w_pool).reshape(B, S, D_POOL) + b_pool
    return y * pool_scale


def conv_ffn(h, w_up, conv_w, conv_b, w_down):
    a = h @ w_up
    a = lax.conv_general_dilated(a, conv_w, window_strides=(1,),
                                 padding=((CONV_WIDTH - 1, 0),),
                                 dimension_numbers=('NWC', 'WIO', 'NWC'),
                                 feature_group_count=2 * D_FF) + conv_b
    gate, val = a[..., :D_FF], a[..., D_FF:]
    return (jax.nn.silu(gate) * val) @ w_down


def setup_inputs(seed: int = 0) -> dict:
    key = jax.random.key(seed)
    ks = jax.random.split(key, 16)
    f32 = jnp.float32

    def nrm(k, shape, scale):
        return jax.random.normal(k, shape, f32) * scale

    return {
        "x": nrm(ks[0], (BATCH, SEQ, D_MODEL), 1.0),
        "norm_mix_pre": 1.0 + nrm(ks[1], (DEPTH, D_MODEL), 0.05),
        "w_in": nrm(ks[2], (DEPTH, D_MODEL, D_IN_PROJ), D_MODEL ** -0.5),
        "w_pool": nrm(ks[3], (DEPTH, N_POOL_GROUPS, POOL_GROUP_DIM, POOL_GROUP_DIM), POOL_GROUP_DIM ** -0.5),
        "b_pool": nrm(ks[4], (DEPTH, D_POOL), 0.01),
        "pool_scale": 1.0 + nrm(ks[5], (DEPTH, D_POOL), 0.1),
        "w_out": nrm(ks[6], (DEPTH, D_MIX, D_MODEL), D_MIX ** -0.5),
        "norm_mix_post": 1.0 + nrm(ks[7], (DEPTH, D_MODEL), 0.05),
        "norm_ffn_pre": 1.0 + nrm(ks[8], (DEPTH, D_MODEL), 0.05),
        "w_up": nrm(ks[9], (DEPTH, D_MODEL, 2 * D_FF), D_MODEL ** -0.5),
        "conv_w": nrm(ks[10], (DEPTH, CONV_WIDTH, 1, 2 * D_FF), CONV_WIDTH ** -0.5),
        "conv_b": nrm(ks[11], (DEPTH, 2 * D_FF), 0.01),
        "w_down": nrm(ks[12], (DEPTH, D_FF, D_MODEL), D_FF ** -0.5),
        "norm_ffn_post": 1.0 + nrm(ks[13], (DEPTH, D_MODEL), 0.05),
    }


def reference(x, norm_mix_pre, w_in, w_pool, b_pool, pool_scale, w_out, norm_mix_post,
              norm_ffn_pre, w_up, conv_w, conv_b, w_down, norm_ffn_post):
    B, S, _ = x.shape
    for l in range(DEPTH):
        h = rmsnorm(x, norm_mix_pre[l])
        proj = h @ w_in[l]
        q = proj[..., :D_ATTN].reshape(B, S, N_ATTN_HEADS, HEAD_DIM)
        k = proj[..., D_ATTN:2 * D_ATTN].reshape(B, S, N_ATTN_HEADS, HEAD_DIM)
        v = proj[..., 2 * D_ATTN:3 * D_ATTN].reshape(B, S, N_ATTN_HEADS, HEAD_DIM)
        u = proj[..., 3 * D_ATTN:]
        q = rope(q).transpose(0, 2, 1, 3)
        k = rope(k).transpose(0, 2, 1, 3)
        v = v.transpose(0, 2, 1, 3)
        y_attn = moba_attention(q, k, v)
        y_pool = pool_mixer(u, w_pool[l], b_pool[l], pool_scale[l])
        mix = jnp.concatenate([y_attn, y_pool], axis=-1) @ w_out[l]
        x = x + rmsnorm(mix, norm_mix_post[l])
        h = rmsnorm(x, norm_ffn_pre[l])
        f = conv_ffn(h, w_up[l], conv_w[l], conv_b[l], w_down[l])
        x = x + rmsnorm(f, norm_ffn_post[l])
    return x
```

```python
import functools
import math

import jax
import jax.numpy as jnp
from jax import lax
from jax.experimental import pallas as pl
from jax.experimental.pallas import tpu as pltpu

D_MODEL = 1024
N_ATTN_HEADS = 8
HEAD_DIM = 64
D_ATTN = N_ATTN_HEADS * HEAD_DIM
D_POOL = D_MODEL - D_ATTN
POOL_WINDOWS = (2, 4, 8, 16)
POOL_GROUP_DIM = D_POOL // len(POOL_WINDOWS)
D_IN_PROJ = 3 * D_ATTN + D_POOL
MOBA_BLOCK = 256
MOBA_TOPK = 3
ROPE_THETA = 10000.0
D_FF = ((8 * D_MODEL // 3 + 127) // 128) * 128
CONV_WIDTH = 3
RMS_EPS = 1e-6
NEG_INF = -1e30

V7X_LANES = 128
V7X_SUBLANES = 8
V7X_MXU_DIM = 256
V7X_VMEM_BYTES = 64 * 1024 * 1024

SEQ_TILE = 512
HEADS_PER_STEP = V7X_MXU_DIM // HEAD_DIM
FF_CHUNK = 256
POOL_HALO = max(POOL_WINDOWS)
CONV_HALO = V7X_SUBLANES

F32 = jnp.float32
BF16 = jnp.bfloat16
_NT = (((1,), (1,)), ((), ()))


def _vmem_limit(block_bytes, scratch_bytes, temp_bytes):
    need = 2 * block_bytes + scratch_bytes + temp_bytes
    assert need <= V7X_VMEM_BYTES, need
    return int(need)


def _nbytes(shape, dtype):
    return math.prod(shape) * jnp.dtype(dtype).itemsize


def _rms_scale(v):
    return lax.rsqrt(jnp.mean(v * v, axis=-1, keepdims=True) + RMS_EPS)


def _mix_in_kernel(x_ref, g_ref, w_ref, cos_ref, sin_ref, wp_ref, bp_ref, ps_ref,
                   q_ref, k_ref, vt_ref, yp_ref, ubuf):
    s = pl.program_id(1)
    ts = x_ref.shape[1]
    xf = x_ref[0]
    hn = (xf * _rms_scale(xf) * g_ref[...]).astype(BF16)

    reps = D_ATTN // cos_ref.shape[1]
    cos = jnp.concatenate([cos_ref[...]] * reps, axis=1)
    sin = jnp.concatenate([sin_ref[...]] * reps, axis=1)
    lane = lax.broadcasted_iota(jnp.int32, (ts, D_ATTN), 1)
    first_half = (lane % HEAD_DIM) < (HEAD_DIM // 2)

    def rope(t):
        partner = jnp.where(first_half,
                            pltpu.roll(t, D_ATTN - HEAD_DIM // 2, axis=1),
                            pltpu.roll(t, HEAD_DIM // 2, axis=1))
        return t * cos + partner * sin

    qf = jnp.dot(hn, w_ref[:, 0:D_ATTN], preferred_element_type=F32)
    q_ref[0] = (rope(qf) * (HEAD_DIM ** -0.5)).astype(BF16)
    kf = jnp.dot(hn, w_ref[:, D_ATTN:2 * D_ATTN], preferred_element_type=F32)
    k_ref[0] = rope(kf).astype(BF16)
    vf = jnp.dot(hn, w_ref[:, 2 * D_ATTN:3 * D_ATTN], preferred_element_type=F32)
    vt_ref[0] = vf.T.astype(BF16)

    u = jnp.dot(hn, w_ref[:, 3 * D_ATTN:], preferred_element_type=F32)

    @pl.when(s == 0)
    def _():
        ubuf[0:POOL_HALO, :] = jnp.zeros((POOL_HALO, D_POOL), F32)

    @pl.when(s > 0)
    def _():
        ubuf[0:POOL_HALO, :] = ubuf[ts:ts + POOL_HALO, :]

    ubuf[POOL_HALO:POOL_HALO + ts, :] = u
    t_pos = s * ts + lax.broadcasted_iota(jnp.int32, (ts, POOL_GROUP_DIM), 0)
    for g, w in enumerate(POOL_WINDOWS):
        cols = slice(g * POOL_GROUP_DIM, (g + 1) * POOL_GROUP_DIM)
        ug = u[:, cols]
        acc = ug
        for back in range(1, w):
            acc = acc + ubuf[pl.ds(POOL_HALO - back, ts), cols]
        cnt = jnp.minimum(t_pos + 1, w).astype(F32)
        diff = (acc / cnt - ug).astype(BF16)
        yg = jnp.dot(diff, wp_ref[g], preferred_element_type=F32)
        yg = (yg + bp_ref[:, cols]) * ps_ref[:, cols]
        yp_ref[0, :, cols] = yg.astype(BF16)


def _mix_in(x, g_pre, w_in, cos_t, sin_t, w_pool, b_pool, pool_scale):
    B, S, _ = x.shape
    ts = SEQ_TILE
    blocks = (_nbytes((ts, D_MODEL), F32) + _nbytes((D_MODEL, D_IN_PROJ), BF16)
              + 2 * _nbytes((ts, V7X_LANES), F32) + _nbytes(w_pool.shape, BF16)
              + 4 * _nbytes((ts, D_ATTN), BF16))
    scratch = _nbytes((POOL_HALO + ts, D_POOL), F32)
    temps = 12 * _nbytes((ts, D_ATTN), F32)
    row = lambda b, s: (0, 0)
    return pl.pallas_call(
        _mix_in_kernel,
        name="mix_in",
        grid=(B, S // ts),
        in_specs=[
            pl.BlockSpec((1, ts, D_MODEL), lambda b, s: (b, s, 0)),
            pl.BlockSpec((1, D_MODEL), row),
            pl.BlockSpec((D_MODEL, D_IN_PROJ), row),
            pl.BlockSpec((ts, V7X_LANES), lambda b, s: (s, 0)),
            pl.BlockSpec((ts, V7X_LANES), lambda b, s: (s, 0)),
            pl.BlockSpec(w_pool.shape, lambda b, s: (0, 0, 0)),
            pl.BlockSpec((1, D_POOL), row),
            pl.BlockSpec((1, D_POOL), row),
        ],
        out_specs=[
            pl.BlockSpec((1, ts, D_ATTN), lambda b, s: (b, s, 0)),
            pl.BlockSpec((1, ts, D_ATTN), lambda b, s: (b, s, 0)),
            pl.BlockSpec((1, D_ATTN, ts), lambda b, s: (b, 0, s)),
            pl.BlockSpec((1, ts, D_POOL), lambda b, s: (b, s, 0)),
        ],
        out_shape=[
            jax.ShapeDtypeStruct((B, S, D_ATTN), BF16),
            jax.ShapeDtypeStruct((B, S, D_ATTN), BF16),
            jax.ShapeDtypeStruct((B, D_ATTN, S), BF16),
            jax.ShapeDtypeStruct((B, S, D_POOL), BF16),
        ],
        scratch_shapes=[pltpu.VMEM((POOL_HALO + ts, D_POOL), F32)],
        compiler_params=pltpu.CompilerParams(
            dimension_semantics=("parallel", "arbitrary"),
            vmem_limit_bytes=_vmem_limit(blocks, scratch, temps)),
    )(x, g_pre, w_in, cos_t, sin_t, w_pool, b_pool, pool_scale)


def _moba_kernel(q_ref, k_ref, vt_ref, o_ref, s_scr, p_scr):
    S = q_ref.shape[1]
    slab = q_ref.shape[2]
    blk = MOBA_BLOCK
    nb = S // blk
    n_sel = min(MOBA_TOPK, nb - 1)

    lane = lax.broadcasted_iota(jnp.int32, (1, slab), 1)
    kmean = jnp.concatenate(
        [jnp.sum(k_ref[0, j * blk:(j + 1) * blk, :].astype(F32), axis=0, keepdims=True)
         for j in range(nb)], axis=0) * (1.0 / blk)

    blk_id = lax.broadcasted_iota(jnp.int32, (nb, S), 0)
    q_blk = lax.broadcasted_iota(jnp.int32, (nb, S), 1) // blk
    past = blk_id < q_blk
    key_row = lax.broadcasted_iota(jnp.int32, (blk, blk), 0)
    qry_col = lax.broadcasted_iota(jnp.int32, (blk, blk), 1)
    causal = key_row <= qry_col

    def head_body(hh, carry):
        hmask = (lane // HEAD_DIM) == hh
        km = jnp.where(hmask, kmean, 0.0)
        km_hi = km.astype(BF16)
        km_lo = (km - km_hi.astype(F32)).astype(BF16)
        g2 = lax.dot_general(jnp.concatenate([km_hi, km_lo], axis=0), q_ref[0], _NT,
                             preferred_element_type=F32)
        gate = jnp.where(past, g2[0:nb] + g2[nb:2 * nb], -jnp.inf)
        rank = jnp.zeros((nb, S), jnp.int32)
        for other in range(nb):
            row = gate[other:other + 1, :]
            beats = jnp.where(row > gate, 1,
                              jnp.where(jnp.logical_and(row == gate, other < blk_id), 1, 0))
            rank = rank + beats
        bias = jnp.where(jnp.logical_and(rank < n_sel, past), 0.0, NEG_INF)

        row0 = pl.multiple_of(hh * HEAD_DIM, HEAD_DIM)
        for i in range(nb):
            qcols = slice(i * blk, (i + 1) * blk)
            qi = q_ref[0, qcols, :]
            qm = jnp.where(hmask, qi, jnp.zeros_like(qi))
            m = None
            for j in range(i + 1):
                rows = slice(j * blk, (j + 1) * blk)
                sj = lax.dot_general(k_ref[0, rows, :], qm, _NT,
                                     preferred_element_type=F32)
                if j < i:
                    sj = sj + bias[j:j + 1, qcols]
                else:
                    sj = jnp.where(causal, sj, NEG_INF)
                s_scr[rows, :] = sj
                mj = jnp.max(sj, axis=0, keepdims=True)
                m = mj if m is None else jnp.maximum(m, mj)
            l = jnp.zeros((1, blk), F32)
            for j in range(i + 1):
                rows = slice(j * blk, (j + 1) * blk)
                p = jnp.exp(s_scr[rows, :] - m)
                l = l + jnp.sum(p, axis=0, keepdims=True)
                p_scr[rows, :] = p.astype(BF16)
            kv = (i + 1) * blk
            vt = vt_ref[0, pl.ds(row0, HEAD_DIM), 0:kv]
            o = jnp.dot(vt, p_scr[0:kv, :], preferred_element_type=F32)
            o_ref[0, pl.ds(row0, HEAD_DIM), qcols] = (o / l).astype(BF16)
        return carry

    lax.fori_loop(0, slab // HEAD_DIM, head_body, 0)


def _moba(q, k, vt):
    B, S, _ = q.shape
    slab = HEADS_PER_STEP * HEAD_DIM
    blocks = 4 * _nbytes((S, slab), BF16)
    scratch = _nbytes((S, MOBA_BLOCK), F32) + _nbytes((S, MOBA_BLOCK), BF16)
    temps = 8 * _nbytes((MOBA_BLOCK, MOBA_BLOCK), F32) + 8 * _nbytes((S // MOBA_BLOCK * 2, S), F32)
    return pl.pallas_call(
        _moba_kernel,
        name="moba",
        grid=(B, D_ATTN // slab),
        in_specs=[
            pl.BlockSpec((1, S, slab), lambda b, c: (b, 0, c)),
            pl.BlockSpec((1, S, slab), lambda b, c: (b, 0, c)),
            pl.BlockSpec((1, slab, S), lambda b, c: (b, c, 0)),
        ],
        out_specs=pl.BlockSpec((1, slab, S), lambda b, c: (b, c, 0)),
        out_shape=jax.ShapeDtypeStruct((B, D_ATTN, S), BF16),
        scratch_shapes=[pltpu.VMEM((S, MOBA_BLOCK), F32), pltpu.VMEM((S, MOBA_BLOCK), BF16)],
        compiler_params=pltpu.CompilerParams(
            dimension_semantics=("parallel", "parallel"),
            vmem_limit_bytes=_vmem_limit(blocks, scratch, temps)),
    )(q, k, vt)


def _mix_out_kernel(yt_ref, yp_ref, x_ref, wo_ref, g_ref, o_ref):
    ya = yt_ref[0].T
    mix = jnp.dot(ya, wo_ref[0:D_ATTN, :], preferred_element_type=F32)
    mix = mix + jnp.dot(yp_ref[0], wo_ref[D_ATTN:, :], preferred_element_type=F32)
    o_ref[0] = x_ref[0] + mix * _rms_scale(mix) * g_ref[...]


def _mix_out(yt, yp, x, w_out, g_post):
    B, S, _ = x.shape
    ts = SEQ_TILE
    blocks = (2 * _nbytes((ts, D_ATTN), BF16) + 2 * _nbytes((ts, D_MODEL), F32)
              + _nbytes(w_out.shape, BF16))
    temps = 4 * _nbytes((ts, D_MODEL), F32)
    row = lambda b, s: (0, 0)
    return pl.pallas_call(
        _mix_out_kernel,
        name="mix_out",
        grid=(B, S // ts),
        in_specs=[
            pl.BlockSpec((1, D_ATTN, ts), lambda b, s: (b, 0, s)),
            pl.BlockSpec((1, ts, D_POOL), lambda b, s: (b, s, 0)),
            pl.BlockSpec((1, ts, D_MODEL), lambda b, s: (b, s, 0)),
            pl.BlockSpec(w_out.shape, row),
            pl.BlockSpec((1, D_MODEL), row),
        ],
        out_specs=pl.BlockSpec((1, ts, D_MODEL), lambda b, s: (b, s, 0)),
        out_shape=jax.ShapeDtypeStruct((B, S, D_MODEL), F32),
        compiler_params=pltpu.CompilerParams(
            dimension_semantics=("parallel", "parallel"),
            vmem_limit_bytes=_vmem_limit(blocks, 0, temps)),
    )(yt, yp, x, w_out, g_post)


def _conv_ffn_kernel(x_ref, gpre_ref, wup_ref, cw_ref, cb_ref, wd_ref, gpost_ref, o_ref,
                     h_scr, abuf, carry, acc):
    r = pl.program_id(1)
    rows = x_ref.shape[1]
    n_chunks = wup_ref.shape[0]
    tf = wd_ref.shape[1]
    x1 = x_ref[0]
    h_scr[...] = (x1 * _rms_scale(x1) * gpre_ref[...]).astype(BF16)
    acc[...] = jnp.zeros_like(acc)

    @pl.when(r == 0)
    def _():
        carry[...] = jnp.zeros_like(carry)

    def chunk(c, _):
        a = jnp.dot(h_scr[...], wup_ref[c], preferred_element_type=F32)
        abuf[0:CONV_HALO, :] = carry[c]
        abuf[CONV_HALO:CONV_HALO + rows, :] = a
        carry[c] = a[rows - CONV_HALO:rows, :]
        cw = cw_ref[c]
        conv = a * cw[CONV_WIDTH - 1:CONV_WIDTH, :] + cb_ref[c]
        for back in range(1, CONV_WIDTH):
            tap = CONV_WIDTH - 1 - back
            conv = conv + abuf[pl.ds(CONV_HALO - back, rows), :] * cw[tap:tap + 1, :]
        gate = conv[:, 0:tf]
        val = conv[:, tf:2 * tf]
        z = (gate * (1.0 / (1.0 + jnp.exp(-gate))) * val).astype(BF16)
        acc[...] += jnp.dot(z, wd_ref[c], preferred_element_type=F32)
        return 0

    lax.fori_loop(0, n_chunks, chunk, 0)
    f = acc[...]
    o_ref[0] = x1 + f * _rms_scale(f) * gpost_ref[...]


def _conv_ffn(x1, g_pre, wup, cw, cb, wd, g_post):
    B, S, _ = x1.shape
    rows = SEQ_TILE
    n_chunks, _, two_tf = wup.shape
    blocks = (2 * _nbytes((rows, D_MODEL), F32) + _nbytes(wup.shape, BF16)
              + _nbytes(wd.shape, BF16) + _nbytes((n_chunks, V7X_SUBLANES, two_tf), F32) * 2)
    scratch = (_nbytes((rows, D_MODEL), BF16) + _nbytes((CONV_HALO + rows, two_tf), F32)
               + _nbytes((n_chunks, CONV_HALO, two_tf), F32) + _nbytes((rows, D_MODEL), F32))
    temps = 6 * _nbytes((rows, two_tf), F32)
    row = lambda b, r: (0, 0)
    whole3 = lambda b, r: (0, 0, 0)
    return pl.pallas_call(
        _conv_ffn_kernel,
        name="conv_ffn",
        grid=(B, S // rows),
        in_specs=[
            pl.BlockSpec((1, rows, D_MODEL), lambda b, r: (b, r, 0)),
            pl.BlockSpec((1, D_MODEL), row),
            pl.BlockSpec(wup.shape, whole3),
            pl.BlockSpec(cw.shape, whole3),
            pl.BlockSpec(cb.shape, whole3),
            pl.BlockSpec(wd.shape, whole3),
            pl.BlockSpec((1, D_MODEL), row),
        ],
        out_specs=pl.BlockSpec((1, rows, D_MODEL), lambda b, r: (b, r, 0)),
        out_shape=jax.ShapeDtypeStruct((B, S, D_MODEL), F32),
        scratch_shapes=[
            pltpu.VMEM((rows, D_MODEL), BF16),
            pltpu.VMEM((CONV_HALO + rows, two_tf), F32),
            pltpu.VMEM((n_chunks, CONV_HALO, two_tf), F32),
            pltpu.VMEM((rows, D_MODEL), F32),
        ],
        compiler_params=pltpu.CompilerParams(
            dimension_semantics=("parallel", "arbitrary"),
            vmem_limit_bytes=_vmem_limit(blocks, scratch, temps)),
    )(x1, g_pre, wup, cw, cb, wd, g_post)


def _rope_tables(seq_len):
    inv_freq = ROPE_THETA ** (-jnp.arange(0, HEAD_DIM, 2, dtype=F32) / HEAD_DIM)
    ang = jnp.arange(seq_len, dtype=F32)[:, None] * inv_freq[None, :]
    cos, sin = jnp.cos(ang), jnp.sin(ang)
    reps = V7X_LANES // HEAD_DIM
    cos_t = jnp.tile(jnp.concatenate([cos, cos], axis=1), (1, reps))
    sin_t = jnp.tile(jnp.concatenate([-sin, sin], axis=1), (1, reps))
    return cos_t, sin_t


def _ffn_weight_layout(w_up, conv_w, conv_b, w_down):
    tf = FF_CHUNK
    nc = D_FF // tf
    wup = w_up.astype(BF16).reshape(D_MODEL, 2, nc, tf).transpose(2, 0, 1, 3).reshape(nc, D_MODEL, 2 * tf)
    cw = conv_w.reshape(CONV_WIDTH, 2, nc, tf).transpose(2, 0, 1, 3).reshape(nc, CONV_WIDTH, 2 * tf)
    cb = conv_b.reshape(2, nc, tf).transpose(1, 0, 2).reshape(nc, 1, 2 * tf)
    wd = w_down.astype(BF16).reshape(nc, tf, D_MODEL)
    return wup, cw, cb, wd


def kernel(x, norm_mix_pre, w_in, w_pool, b_pool, pool_scale, w_out, norm_mix_post,
           norm_ffn_pre, w_up, conv_w, conv_b, w_down, norm_ffn_post):
    B, S, D = x.shape
    assert D == D_MODEL and S % SEQ_TILE == 0 and S % MOBA_BLOCK == 0 and D_FF % FF_CHUNK == 0
    depth = w_in.shape[0]
    cos_t, sin_t = _rope_tables(S)
    for l in range(depth):
        q, k, vt, yp = _mix_in(
            x, norm_mix_pre[l][None, :], w_in[l].astype(BF16), cos_t, sin_t,
            w_pool[l].astype(BF16), b_pool[l][None, :], pool_scale[l][None, :])
        yt = _moba(q, k, vt)
        x1 = _mix_out(yt, yp, x, w_out[l].astype(BF16), norm_mix_post[l][None, :])
        wup, cw, cb, wd = _ffn_weight_layout(w_up[l], conv_w[l], conv_b[l], w_down[l])
        x = _conv_ffn(x1, norm_ffn_pre[l][None, :], wup, cw, cb, wd, norm_ffn_post[l][None, :])
    return x
```

```python
import math

import jax
import jax.numpy as jnp
from jax import lax
from jax.experimental import pallas as pl
from jax.experimental.pallas import tpu as pltpu

D_MODEL = 1024
N_ATTN_HEADS = 8
HEAD_DIM = 64
D_ATTN = N_ATTN_HEADS * HEAD_DIM
D_POOL = D_MODEL - D_ATTN
POOL_WINDOWS = (2, 4, 8, 16)
POOL_GROUP_DIM = D_POOL // len(POOL_WINDOWS)
D_IN_PROJ = 3 * D_ATTN + D_POOL
MOBA_BLOCK = 256
MOBA_TOPK = 3
ROPE_THETA = 10000.0
D_FF = ((8 * D_MODEL // 3 + 127) // 128) * 128
CONV_WIDTH = 3
RMS_EPS = 1e-6
NEG_INF = -1e30
LOG2_E = 1.4426950408889634

V7X_LANES = 128
V7X_SUBLANES = 8
V7X_BF16_ROWS = 16
V7X_MXU_DIM = 256
V7X_VMEM_BYTES = 64 * 1024 * 1024

SEQ_TILE = 512
HEADS_PER_STEP = V7X_MXU_DIM // HEAD_DIM
V7X_NUM_MXU = 2
MOBA_QUERY_TILE = V7X_NUM_MXU * MOBA_BLOCK
FF_CHUNK = 256
POOL_HALO = max(POOL_WINDOWS)
POOL_PAD = V7X_SUBLANES
CONV_HALO = V7X_SUBLANES

F32 = jnp.float32
BF16 = jnp.bfloat16


def _vmem_limit(block_bytes, scratch_bytes, temp_bytes):
    need = 2 * block_bytes + scratch_bytes + temp_bytes
    assert need <= V7X_VMEM_BYTES, need
    return int(need)


def _nbytes(shape, dtype):
    return math.prod(shape) * jnp.dtype(dtype).itemsize


def _rms_scale(v):
    return lax.rsqrt(jnp.mean(v * v, axis=-1, keepdims=True) + RMS_EPS)


def _mix_in_kernel(x_ref, g_ref, w_ref, cos_ref, sin_ref, wp_ref, bp_ref, ps_ref,
                   qt_ref, k_ref, vt_ref, yp_ref, ubuf, lvl1, lvl2, lvl3):
    s = pl.program_id(1)
    ts = x_ref.shape[1]
    top = POOL_PAD + POOL_HALO
    gd = POOL_GROUP_DIM

    @pl.when(s == 0)
    def _():
        ubuf[ts + POOL_PAD:ts + top, :] = jnp.zeros((POOL_HALO, D_POOL), F32)

    xf = x_ref[0]
    hn = (xf * _rms_scale(xf) * g_ref[...]).astype(BF16)

    u = jnp.dot(hn, w_ref[:, 3 * D_ATTN:], preferred_element_type=F32)
    ubuf[POOL_PAD:top, :] = ubuf[ts + POOL_PAD:ts + top, :]
    ubuf[0:POOL_PAD, :] = jnp.zeros((POOL_PAD, D_POOL), F32)
    lvl1[0:POOL_PAD, :] = jnp.zeros((POOL_PAD, D_POOL), F32)
    lvl2[0:POOL_PAD, :] = jnp.zeros((POOL_PAD, D_POOL - gd), F32)
    ubuf[top:top + ts, :] = u
    ext = ts + POOL_HALO
    s2 = ubuf[POOL_PAD:POOL_PAD + ext, :] + ubuf[pl.ds(POOL_PAD - 1, ext), :]
    lvl1[POOL_PAD:POOL_PAD + ext, :] = s2
    s4 = s2[:, gd:] + lvl1[pl.ds(POOL_PAD - 2, ext), gd:]
    lvl2[POOL_PAD:POOL_PAD + ext, :] = s4
    s8 = s4[:, gd:] + lvl2[pl.ds(POOL_PAD - 4, ext), gd:]
    lvl3[POOL_PAD:POOL_PAD + ext, :] = s8
    s16 = s8[POOL_HALO:, gd:] + lvl3[top - 8:top - 8 + ts, gd:]
    sums = (s2[POOL_HALO:, 0:gd], s4[POOL_HALO:, 0:gd], s8[POOL_HALO:, 0:gd], s16)
    t_pos = s * ts + lax.broadcasted_iota(jnp.int32, (ts, gd), 0)
    for g, w in enumerate(POOL_WINDOWS):
        cols = slice(g * gd, (g + 1) * gd)
        cnt = jnp.minimum(t_pos + 1, w).astype(F32)
        diff = (sums[g] / cnt - u[:, cols]).astype(BF16)
        yg = jnp.dot(diff, wp_ref[g], preferred_element_type=F32)
        yg = (yg + bp_ref[:, cols]) * ps_ref[:, cols]
        yp_ref[0, :, cols] = yg.astype(BF16)

    reps = D_ATTN // cos_ref.shape[1]
    cos = jnp.concatenate([cos_ref[...]] * reps, axis=1)
    sin = jnp.concatenate([sin_ref[...]] * reps, axis=1)
    lane = lax.broadcasted_iota(jnp.int32, (ts, D_ATTN), 1)
    first_half = (lane % HEAD_DIM) < (HEAD_DIM // 2)

    def rope(t):
        partner = jnp.where(first_half,
                            pltpu.roll(t, D_ATTN - HEAD_DIM // 2, axis=1),
                            pltpu.roll(t, HEAD_DIM // 2, axis=1))
        return t * cos + partner * sin

    qf = jnp.dot(hn, w_ref[:, 0:D_ATTN], preferred_element_type=F32)
    qt_ref[0] = (rope(qf) * (HEAD_DIM ** -0.5 * LOG2_E)).T.astype(BF16)
    kf = jnp.dot(hn, w_ref[:, D_ATTN:2 * D_ATTN], preferred_element_type=F32)
    k_ref[0] = rope(kf).astype(BF16)
    vf = jnp.dot(hn, w_ref[:, 2 * D_ATTN:3 * D_ATTN], preferred_element_type=F32)
    vt_ref[0] = vf.T.astype(BF16)


def _mix_in(x, g_pre, w_in, cos_t, sin_t, w_pool, b_pool, pool_scale):
    B, S, _ = x.shape
    ts = SEQ_TILE
    lvl_rows = POOL_PAD + POOL_HALO + ts
    gd = POOL_GROUP_DIM
    blocks = (_nbytes((ts, D_MODEL), F32) + _nbytes((D_MODEL, D_IN_PROJ), BF16)
              + 2 * _nbytes((ts, V7X_LANES), F32) + _nbytes(w_pool.shape, BF16)
              + 4 * _nbytes((ts, D_ATTN), BF16))
    scratch = _nbytes((lvl_rows, 4 * D_POOL - 3 * gd), F32)
    temps = 12 * _nbytes((ts, D_ATTN), F32)
    row = lambda b, s: (0, 0)
    return pl.pallas_call(
        _mix_in_kernel,
        name="mix_in",
        grid=(B, S // ts),
        in_specs=[
            pl.BlockSpec((1, ts, D_MODEL), lambda b, s: (b, s, 0)),
            pl.BlockSpec((1, D_MODEL), row),
            pl.BlockSpec((D_MODEL, D_IN_PROJ), row),
            pl.BlockSpec((ts, V7X_LANES), lambda b, s: (s, 0)),
            pl.BlockSpec((ts, V7X_LANES), lambda b, s: (s, 0)),
            pl.BlockSpec(w_pool.shape, lambda b, s: (0, 0, 0)),
            pl.BlockSpec((1, D_POOL), row),
            pl.BlockSpec((1, D_POOL), row),
        ],
        out_specs=[
            pl.BlockSpec((1, D_ATTN, ts), lambda b, s: (b, 0, s)),
            pl.BlockSpec((1, ts, D_ATTN), lambda b, s: (b, s, 0)),
            pl.BlockSpec((1, D_ATTN, ts), lambda b, s: (b, 0, s)),
            pl.BlockSpec((1, ts, D_POOL), lambda b, s: (b, s, 0)),
        ],
        out_shape=[
            jax.ShapeDtypeStruct((B, D_ATTN, S), BF16),
            jax.ShapeDtypeStruct((B, S, D_ATTN), BF16),
            jax.ShapeDtypeStruct((B, D_ATTN, S), BF16),
            jax.ShapeDtypeStruct((B, S, D_POOL), BF16),
        ],
        scratch_shapes=[
            pltpu.VMEM((lvl_rows, D_POOL), F32),
            pltpu.VMEM((lvl_rows, D_POOL), F32),
            pltpu.VMEM((lvl_rows, D_POOL - gd), F32),
            pltpu.VMEM((lvl_rows, D_POOL - 2 * gd), F32),
        ],
        compiler_params=pltpu.CompilerParams(
            dimension_semantics=("parallel", "arbitrary"),
            vmem_limit_bytes=_vmem_limit(blocks, scratch, temps)),
    )(x, g_pre, w_in, cos_t, sin_t, w_pool, b_pool, pool_scale)


def _moba_kernel(qt_ref, k_ref, vt_ref, o_ref, s_scr0, s_scr1, p_scr0, p_scr1,
                 diag_lo, diag_hi):
    slab = qt_ref.shape[1]
    S = qt_ref.shape[2]
    blk = MOBA_BLOCK
    nb = S // blk
    n_sel = min(MOBA_TOPK, nb - 1)
    sq = MOBA_QUERY_TILE
    per = sq // blk

    lane = lax.broadcasted_iota(jnp.int32, (1, slab), 1)
    kmean = jnp.concatenate(
        [jnp.sum(k_ref[0, j * blk:(j + 1) * blk, :].astype(F32), axis=0, keepdims=True)
         for j in range(nb)], axis=0) * (1.0 / blk)

    blk_id = lax.broadcasted_iota(jnp.int32, (nb, S), 0)
    q_blk = lax.broadcasted_iota(jnp.int32, (nb, S), 1) // blk
    past = blk_id < q_blk
    key_row = lax.broadcasted_iota(jnp.int32, (blk, sq), 0)
    qry_col = lax.broadcasted_iota(jnp.int32, (blk, sq), 1)
    diag_lo[...] = jnp.where(key_row <= qry_col, 0.0, NEG_INF)
    diag_hi[...] = jnp.where(key_row + blk <= qry_col, 0.0, NEG_INF)
    second_block = lax.broadcasted_iota(jnp.int32, (1, sq), 1) >= blk
    row_base = jnp.minimum(pl.program_id(0), 0)
    n_heads = slab // HEAD_DIM

    gate_lhs = []
    for h in range(n_heads):
        km = jnp.where((lane // HEAD_DIM) == h, kmean, 0.0)
        km_hi = km.astype(BF16)
        gate_lhs += [km_hi, (km - km_hi.astype(F32)).astype(BF16)]
    g2 = jnp.dot(jnp.concatenate(gate_lhs, axis=0), qt_ref[0], preferred_element_type=F32)

    def selection_bias(h):
        g_hi = g2[2 * h * nb:(2 * h + 1) * nb]
        g_lo = g2[(2 * h + 1) * nb:(2 * h + 2) * nb]
        gate = jnp.where(past, g_hi + g_lo, -jnp.inf)
        rank = jnp.zeros((nb, S), jnp.int32)
        for other in range(nb):
            row = gate[other:other + 1, :]
            beats = jnp.where(row > gate, 1,
                              jnp.where(jnp.logical_and(row == gate, other < blk_id), 1, 0))
            rank = rank + beats
        return jnp.where(jnp.logical_and(rank < n_sel, past), 0.0, NEG_INF)

    class Stage:
        def __init__(self, index, h, t, bias):
            self.h, self.t, self.bias = h, t, bias
            self.s_scr = (s_scr0, s_scr1)[index % 2]
            self.p_scr = (p_scr0, p_scr1)[index % 2]
            self.qcols = slice(t * sq, (t + 1) * sq)
            self.n_kv = per * (t + 1)
            self.head = slice(h * HEAD_DIM, (h + 1) * HEAD_DIM)
            pieces = []
            if h > 0:
                pieces.append(jnp.zeros((h * HEAD_DIM, sq), BF16))
            pieces.append(qt_ref[0, self.head, self.qcols])
            if h < n_heads - 1:
                pieces.append(jnp.zeros(((n_heads - 1 - h) * HEAD_DIM, sq), BF16))
            self.qm = jnp.concatenate(pieces, axis=0)
            self.m = None

        def row_bias(self, j):
            if j == per * self.t:
                return jnp.where(second_block, self.bias[j:j + 1, self.qcols], 0.0)
            if j == per * self.t + 1:
                return None
            return self.bias[j:j + 1, self.qcols]

        def score_block(self, j):
            rows = slice(j * blk, (j + 1) * blk)
            sj = jnp.dot(k_ref[0, rows, :], self.qm, preferred_element_type=F32)
            if j == per * self.t:
                sj = sj + diag_lo[...]
            elif j == per * self.t + 1:
                sj = sj + diag_hi[...]
            self.s_scr[rows, :] = sj
            mj = jnp.max(sj, axis=0, keepdims=True)
            rb = self.row_bias(j)
            if rb is not None:
                mj = mj + rb
            self.m = mj if self.m is None else jnp.maximum(self.m, mj)

        def prob_block(self, j):
            rows = slice(j * blk, (j + 1) * blk)
            rb = self.row_bias(j)
            shift = self.m if rb is None else self.m - rb
            s = self.s_scr[pl.ds(pl.multiple_of(row_base + j * blk, blk), blk), :]
            self.p_scr[rows, :] = jnp.exp2(s - shift).astype(BF16)

        def weighted_values(self):
            kv = self.n_kv * blk
            vt = jnp.concatenate([vt_ref[0, self.head, 0:kv],
                                  jnp.ones((V7X_BF16_ROWS, kv), BF16)], axis=0)
            p = self.p_scr[pl.ds(pl.multiple_of(row_base, blk), kv), :]
            o = jnp.dot(vt, p, preferred_element_type=F32)
            l = o[HEAD_DIM:HEAD_DIM + 1, :]
            o_ref[0, self.head, self.qcols] = (o[0:HEAD_DIM, :] / l).astype(BF16)

    prev = None
    index = 0
    for h in range(n_heads):
        bias = selection_bias(h)
        for t in range(S // sq):
            cur = Stage(index, h, t, bias)
            pending = list(range(prev.n_kv)) if prev is not None else []
            for j in range(cur.n_kv):
                cur.score_block(j)
                if pending:
                    prev.prob_block(pending.pop(0))
            for j in pending:
                prev.prob_block(j)
            if prev is not None:
                prev.weighted_values()
            prev = cur
            index += 1
    for j in range(prev.n_kv):
        prev.prob_block(j)
    prev.weighted_values()


def _moba(qt, k, vt):
    B, _, S = qt.shape
    slab = HEADS_PER_STEP * HEAD_DIM
    blocks = 4 * _nbytes((S, slab), BF16)
    sq = MOBA_QUERY_TILE
    score_bufs = [pltpu.VMEM((S, sq), F32)] * 2 + [pltpu.VMEM((S, sq), BF16)] * 2
    diag_bufs = [pltpu.VMEM((MOBA_BLOCK, sq), F32)] * 2
    scratch = (2 * _nbytes((S, sq), F32) + 2 * _nbytes((S, sq), BF16)
               + 2 * _nbytes((MOBA_BLOCK, sq), F32))
    temps = _nbytes((S, sq), F32) + 8 * _nbytes((S // MOBA_BLOCK * 2, S), F32)
    return pl.pallas_call(
        _moba_kernel,
        name="moba",
        grid=(B, D_ATTN // slab),
        in_specs=[
            pl.BlockSpec((1, slab, S), lambda b, c: (b, c, 0)),
            pl.BlockSpec((1, S, slab), lambda b, c: (b, 0, c)),
            pl.BlockSpec((1, slab, S), lambda b, c: (b, c, 0)),
        ],
        out_specs=pl.BlockSpec((1, slab, S), lambda b, c: (b, c, 0)),
        out_shape=jax.ShapeDtypeStruct((B, D_ATTN, S), BF16),
        scratch_shapes=score_bufs + diag_bufs,
        compiler_params=pltpu.CompilerParams(
            dimension_semantics=("parallel", "parallel"),
            vmem_limit_bytes=_vmem_limit(blocks, scratch, temps)),
    )(qt, k, vt)


def _mix_out_kernel(yt_ref, yp_ref, x_ref, wo_ref, g_ref, o_ref):
    ya = yt_ref[0].T
    mix = jnp.dot(ya, wo_ref[0:D_ATTN, :], preferred_element_type=F32)
    mix = mix + jnp.dot(yp_ref[0], wo_ref[D_ATTN:, :], preferred_element_type=F32)
    o_ref[0] = x_ref[0] + mix * _rms_scale(mix) * g_ref[...]


def _mix_out(yt, yp, x, w_out, g_post):
    B, S, _ = x.shape
    ts = SEQ_TILE
    blocks = (2 * _nbytes((ts, D_ATTN), BF16) + 2 * _nbytes((ts, D_MODEL), F32)
              + _nbytes(w_out.shape, BF16))
    temps = 4 * _nbytes((ts, D_MODEL), F32)
    row = lambda b, s: (0, 0)
    return pl.pallas_call(
        _mix_out_kernel,
        name="mix_out",
        grid=(B, S // ts),
        in_specs=[
            pl.BlockSpec((1, D_ATTN, ts), lambda b, s: (b, 0, s)),
            pl.BlockSpec((1, ts, D_POOL), lambda b, s: (b, s, 0)),
            pl.BlockSpec((1, ts, D_MODEL), lambda b, s: (b, s, 0)),
            pl.BlockSpec(w_out.shape, row),
            pl.BlockSpec((1, D_MODEL), row),
        ],
        out_specs=pl.BlockSpec((1, ts, D_MODEL), lambda b, s: (b, s, 0)),
        out_shape=jax.ShapeDtypeStruct((B, S, D_MODEL), F32),
        compiler_params=pltpu.CompilerParams(
            dimension_semantics=("parallel", "parallel"),
            vmem_limit_bytes=_vmem_limit(blocks, 0, temps)),
    )(yt, yp, x, w_out, g_post)


def _conv_ffn_kernel(x_ref, gpre_ref, wup_ref, cw_ref, cb_ref, wd_ref, gpost_ref, o_ref,
                     h_scr, abuf0, abuf1, carry, z_scr):
    r = pl.program_id(1)
    rows = x_ref.shape[1]
    tf = FF_CHUNK
    n_chunks = D_FF // tf

    @pl.when(r == 0)
    def _():
        carry[...] = jnp.zeros_like(carry)

    x1 = x_ref[0]
    h_scr[...] = (x1 * _rms_scale(x1) * gpre_ref[...]).astype(BF16)

    for c in range(n_chunks):
        abuf = (abuf0, abuf1)[c % 2]
        cols = slice(c * 2 * tf, (c + 1) * 2 * tf)
        a = jnp.dot(h_scr[...], wup_ref[:, cols], preferred_element_type=F32)
        abuf[0:CONV_HALO, :] = carry[c]
        abuf[CONV_HALO:CONV_HALO + rows, :] = a
        carry[c] = a[rows - CONV_HALO:rows, :]
        conv = a * cw_ref[CONV_WIDTH - 1:CONV_WIDTH, cols] + cb_ref[:, cols]
        for back in range(1, CONV_WIDTH):
            tap = CONV_WIDTH - 1 - back
            conv = conv + abuf[pl.ds(CONV_HALO - back, rows), :] * cw_ref[tap:tap + 1, cols]
        gate = conv[:, 0:tf]
        val = conv[:, tf:2 * tf]
        z_scr[:, c * tf:(c + 1) * tf] = (gate * (1.0 / (1.0 + jnp.exp(-gate))) * val).astype(BF16)

    f = jnp.dot(z_scr[...], wd_ref[...], preferred_element_type=F32)
    o_ref[0] = x_ref[0] + f * _rms_scale(f) * gpost_ref[...]


def _conv_ffn(x1, g_pre, wup, cw, cb, wd, g_post):
    B, S, _ = x1.shape
    rows = SEQ_TILE
    tf = FF_CHUNK
    n_chunks = D_FF // tf
    blocks = 2 * _nbytes((rows, D_MODEL), F32) + _nbytes((V7X_SUBLANES + 1, 2 * D_FF), F32)
    resident = _nbytes(wup.shape, BF16) + _nbytes(wd.shape, BF16)
    scratch = (_nbytes((rows, D_MODEL), BF16) + 2 * _nbytes((CONV_HALO + rows, 2 * tf), F32)
               + _nbytes((n_chunks, CONV_HALO, 2 * tf), F32) + _nbytes((rows, D_FF), BF16))
    temps = 6 * _nbytes((rows, 2 * tf), F32) + _nbytes((rows, D_MODEL), F32)
    row = lambda b, r: (0, 0)
    once = pl.Buffered(1)
    return pl.pallas_call(
        _conv_ffn_kernel,
        name="conv_ffn",
        grid=(B, S // rows),
        in_specs=[
            pl.BlockSpec((1, rows, D_MODEL), lambda b, r: (b, r, 0)),
            pl.BlockSpec((1, D_MODEL), row),
            pl.BlockSpec(wup.shape, row, pipeline_mode=once),
            pl.BlockSpec(cw.shape, row),
            pl.BlockSpec(cb.shape, row),
            pl.BlockSpec(wd.shape, row, pipeline_mode=once),
            pl.BlockSpec((1, D_MODEL), row),
        ],
        out_specs=pl.BlockSpec((1, rows, D_MODEL), lambda b, r: (b, r, 0)),
        out_shape=jax.ShapeDtypeStruct((B, S, D_MODEL), F32),
        scratch_shapes=[
            pltpu.VMEM((rows, D_MODEL), BF16),
            pltpu.VMEM((CONV_HALO + rows, 2 * tf), F32),
            pltpu.VMEM((CONV_HALO + rows, 2 * tf), F32),
            pltpu.VMEM((n_chunks, CONV_HALO, 2 * tf), F32),
            pltpu.VMEM((rows, D_FF), BF16),
        ],
        compiler_params=pltpu.CompilerParams(
            dimension_semantics=("parallel", "arbitrary"),
            vmem_limit_bytes=_vmem_limit(blocks, scratch + resident, temps)),
    )(x1, g_pre, wup, cw, cb, wd, g_post)


def _rope_tables(seq_len):
    inv_freq = ROPE_THETA ** (-jnp.arange(0, HEAD_DIM, 2, dtype=F32) / HEAD_DIM)
    ang = jnp.arange(seq_len, dtype=F32)[:, None] * inv_freq[None, :]
    cos, sin = jnp.cos(ang), jnp.sin(ang)
    reps = V7X_LANES // HEAD_DIM
    cos_t = jnp.tile(jnp.concatenate([cos, cos], axis=1), (1, reps))
    sin_t = jnp.tile(jnp.concatenate([-sin, sin], axis=1), (1, reps))
    return cos_t, sin_t


def _pair_gate_value_columns(a):
    lead = a.shape[:-1]
    nc = D_FF // FF_CHUNK
    halves = a.reshape(lead + (2, nc, FF_CHUNK))
    return jnp.swapaxes(halves, -3, -2).reshape(lead + (2 * D_FF,))


def kernel(x, norm_mix_pre, w_in, w_pool, b_pool, pool_scale, w_out, norm_mix_post,
           norm_ffn_pre, w_up, conv_w, conv_b, w_down, norm_ffn_post):
    B, S, D = x.shape
    assert D == D_MODEL and S % SEQ_TILE == 0 and S % MOBA_BLOCK == 0 and D_FF % FF_CHUNK == 0
    depth = w_in.shape[0]
    cos_t, sin_t = _rope_tables(S)
    for l in range(depth):
        qt, k, vt, yp = _mix_in(
            x, norm_mix_pre[l][None, :], w_in[l].astype(BF16), cos_t, sin_t,
            w_pool[l].astype(BF16), b_pool[l][None, :], pool_scale[l][None, :])
        yt = _moba(qt, k, vt)
        x1 = _mix_out(yt, yp, x, w_out[l].astype(BF16), norm_mix_post[l][None, :])
        x = _conv_ffn(
            x1, norm_ffn_pre[l][None, :],
            _pair_gate_value_columns(w_up[l].astype(BF16)),
            _pair_gate_value_columns(conv_w[l].reshape(CONV_WIDTH, 2 * D_FF)),
            _pair_gate_value_columns(conv_b[l][None, :]),
            w_down[l].astype(BF16), norm_ffn_post[l][None, :])
    return x
```

```python
import math

import jax
import jax.numpy as jnp
from jax import lax
from jax.experimental import pallas as pl
from jax.experimental.pallas import tpu as pltpu

D_MODEL = 1024
N_ATTN_HEADS = 8
HEAD_DIM = 64
D_ATTN = N_ATTN_HEADS * HEAD_DIM
D_POOL = D_MODEL - D_ATTN
POOL_WINDOWS = (2, 4, 8, 16)
POOL_GROUP_DIM = D_POOL // len(POOL_WINDOWS)
D_IN_PROJ = 3 * D_ATTN + D_POOL
MOBA_BLOCK = 256
MOBA_TOPK = 3
ROPE_THETA = 10000.0
D_FF = ((8 * D_MODEL // 3 + 127) // 128) * 128
CONV_WIDTH = 3
RMS_EPS = 1e-6
NEG_INF = -1e30
LOG2_E = 1.4426950408889634

V7X_LANES = 128
V7X_SUBLANES = 8
V7X_BF16_ROWS = 16
V7X_MXU_DIM = 256
V7X_VMEM_BYTES = 64 * 1024 * 1024

SEQ_TILE = 512
HEADS_PER_STEP = V7X_MXU_DIM // HEAD_DIM
V7X_NUM_MXU = 2
MOBA_QUERY_TILE = V7X_NUM_MXU * MOBA_BLOCK
FF_CHUNK = 256
POOL_HALO = max(POOL_WINDOWS)
POOL_PAD = V7X_SUBLANES
CONV_HALO = V7X_SUBLANES

F32 = jnp.float32
BF16 = jnp.bfloat16


def _vmem_limit(block_bytes, scratch_bytes, temp_bytes):
    need = 2 * block_bytes + scratch_bytes + temp_bytes
    assert need <= V7X_VMEM_BYTES, need
    return int(need)


def _nbytes(shape, dtype):
    return math.prod(shape) * jnp.dtype(dtype).itemsize


def _rms_scale(v):
    return lax.rsqrt(jnp.mean(v * v, axis=-1, keepdims=True) + RMS_EPS)


def _mix_in_kernel(x_ref, g_ref, w_ref, cos_ref, sin_ref, wp_ref, bp_ref, ps_ref,
                   qt_ref, k_ref, vt_ref, yp_ref, ubuf, lvl1, lvl2, lvl3):
    s = pl.program_id(1)
    ts = x_ref.shape[1]
    top = POOL_PAD + POOL_HALO
    gd = POOL_GROUP_DIM

    @pl.when(s == 0)
    def _():
        ubuf[ts + POOL_PAD:ts + top, :] = jnp.zeros((POOL_HALO, D_POOL), F32)

    xf = x_ref[0]
    hn = (xf * _rms_scale(xf) * g_ref[...]).astype(BF16)

    u = jnp.dot(hn, w_ref[:, 3 * D_ATTN:], preferred_element_type=F32)
    ubuf[POOL_PAD:top, :] = ubuf[ts + POOL_PAD:ts + top, :]
    ubuf[0:POOL_PAD, :] = jnp.zeros((POOL_PAD, D_POOL), F32)
    lvl1[0:POOL_PAD, :] = jnp.zeros((POOL_PAD, D_POOL), F32)
    lvl2[0:POOL_PAD, :] = jnp.zeros((POOL_PAD, D_POOL - gd), F32)
    ubuf[top:top + ts, :] = u
    ext = ts + POOL_HALO
    s2 = ubuf[POOL_PAD:POOL_PAD + ext, :] + ubuf[pl.ds(POOL_PAD - 1, ext), :]
    lvl1[POOL_PAD:POOL_PAD + ext, :] = s2
    s4 = s2[:, gd:] + lvl1[pl.ds(POOL_PAD - 2, ext), gd:]
    lvl2[POOL_PAD:POOL_PAD + ext, :] = s4
    s8 = s4[:, gd:] + lvl2[pl.ds(POOL_PAD - 4, ext), gd:]
    lvl3[POOL_PAD:POOL_PAD + ext, :] = s8
    s16 = s8[POOL_HALO:, gd:] + lvl3[top - 8:top - 8 + ts, gd:]
    sums = (s2[POOL_HALO:, 0:gd], s4[POOL_HALO:, 0:gd], s8[POOL_HALO:, 0:gd], s16)
    t_pos = s * ts + lax.broadcasted_iota(jnp.int32, (ts, gd), 0)
    for g, w in enumerate(POOL_WINDOWS):
        cols = slice(g * gd, (g + 1) * gd)
        cnt = jnp.minimum(t_pos + 1, w).astype(F32)
        diff = (sums[g] / cnt - u[:, cols]).astype(BF16)
        yg = jnp.dot(diff, wp_ref[g], preferred_element_type=F32)
        yg = (yg + bp_ref[:, cols]) * ps_ref[:, cols]
        yp_ref[0, :, cols] = yg.astype(BF16)

    reps = D_ATTN // cos_ref.shape[1]
    cos = jnp.concatenate([cos_ref[...]] * reps, axis=1)
    sin = jnp.concatenate([sin_ref[...]] * reps, axis=1)
    lane = lax.broadcasted_iota(jnp.int32, (ts, D_ATTN), 1)
    first_half = (lane % HEAD_DIM) < (HEAD_DIM // 2)

    def rope(t):
        partner = jnp.where(first_half,
                            pltpu.roll(t, D_ATTN - HEAD_DIM // 2, axis=1),
                            pltpu.roll(t, HEAD_DIM // 2, axis=1))
        return t * cos + partner * sin

    qf = jnp.dot(hn, w_ref[:, 0:D_ATTN], preferred_element_type=F32)
    qt_ref[0] = (rope(qf) * (HEAD_DIM ** -0.5 * LOG2_E)).T.astype(BF16)
    kf = jnp.dot(hn, w_ref[:, D_ATTN:2 * D_ATTN], preferred_element_type=F32)
    k_ref[0] = rope(kf).astype(BF16)
    vf = jnp.dot(hn, w_ref[:, 2 * D_ATTN:3 * D_ATTN], preferred_element_type=F32)
    vt_ref[0] = vf.T.astype(BF16)


def _mix_in(x, g_pre, w_in, cos_t, sin_t, w_pool, b_pool, pool_scale):
    B, S, _ = x.shape
    ts = SEQ_TILE
    lvl_rows = POOL_PAD + POOL_HALO + ts
    gd = POOL_GROUP_DIM
    blocks = (_nbytes((ts, D_MODEL), F32) + _nbytes((D_MODEL, D_IN_PROJ), BF16)
              + 2 * _nbytes((ts, V7X_LANES), F32) + _nbytes(w_pool.shape, BF16)
              + 4 * _nbytes((ts, D_ATTN), BF16))
    scratch = _nbytes((lvl_rows, 4 * D_POOL - 3 * gd), F32)
    temps = 12 * _nbytes((ts, D_ATTN), F32)
    row = lambda b, s: (0, 0)
    return pl.pallas_call(
        _mix_in_kernel,
        name="mix_in",
        grid=(B, S // ts),
        in_specs=[
            pl.BlockSpec((1, ts, D_MODEL), lambda b, s: (b, s, 0)),
            pl.BlockSpec((1, D_MODEL), row),
            pl.BlockSpec((D_MODEL, D_IN_PROJ), row),
            pl.BlockSpec((ts, V7X_LANES), lambda b, s: (s, 0)),
            pl.BlockSpec((ts, V7X_LANES), lambda b, s: (s, 0)),
            pl.BlockSpec(w_pool.shape, lambda b, s: (0, 0, 0)),
            pl.BlockSpec((1, D_POOL), row),
            pl.BlockSpec((1, D_POOL), row),
        ],
        out_specs=[
            pl.BlockSpec((1, D_ATTN, ts), lambda b, s: (b, 0, s)),
            pl.BlockSpec((1, ts, D_ATTN), lambda b, s: (b, s, 0)),
            pl.BlockSpec((1, D_ATTN, ts), lambda b, s: (b, 0, s)),
            pl.BlockSpec((1, ts, D_POOL), lambda b, s: (b, s, 0)),
        ],
        out_shape=[
            jax.ShapeDtypeStruct((B, D_ATTN, S), BF16),
            jax.ShapeDtypeStruct((B, S, D_ATTN), BF16),
            jax.ShapeDtypeStruct((B, D_ATTN, S), BF16),
            jax.ShapeDtypeStruct((B, S, D_POOL), BF16),
        ],
        scratch_shapes=[
            pltpu.VMEM((lvl_rows, D_POOL), F32),
            pltpu.VMEM((lvl_rows, D_POOL), F32),
            pltpu.VMEM((lvl_rows, D_POOL - gd), F32),
            pltpu.VMEM((lvl_rows, D_POOL - 2 * gd), F32),
        ],
        compiler_params=pltpu.CompilerParams(
            dimension_semantics=("parallel", "arbitrary"),
            vmem_limit_bytes=_vmem_limit(blocks, scratch, temps)),
    )(x, g_pre, w_in, cos_t, sin_t, w_pool, b_pool, pool_scale)


def _moba_kernel(qt_ref, k_ref, vt_ref, o_ref, s_scr0, s_scr1, p_scr0, p_scr1,
                 diag_lo, diag_hi):
    slab = qt_ref.shape[1]
    S = qt_ref.shape[2]
    blk = MOBA_BLOCK
    nb = S // blk
    n_sel = min(MOBA_TOPK, nb - 1)
    sq = MOBA_QUERY_TILE
    per = sq // blk

    lane = lax.broadcasted_iota(jnp.int32, (1, slab), 1)
    kmean = jnp.concatenate(
        [jnp.sum(k_ref[0, j * blk:(j + 1) * blk, :].astype(F32), axis=0, keepdims=True)
         for j in range(nb)], axis=0) * (1.0 / blk)

    blk_id = lax.broadcasted_iota(jnp.int32, (nb, S), 0)
    q_blk = lax.broadcasted_iota(jnp.int32, (nb, S), 1) // blk
    past = blk_id < q_blk
    key_row = lax.broadcasted_iota(jnp.int32, (blk, sq), 0)
    qry_col = lax.broadcasted_iota(jnp.int32, (blk, sq), 1)
    diag_lo[...] = jnp.where(key_row <= qry_col, 0.0, NEG_INF)
    diag_hi[...] = jnp.where(key_row + blk <= qry_col, 0.0, NEG_INF)
    second_block = lax.broadcasted_iota(jnp.int32, (1, sq), 1) >= blk
    row_base = jnp.minimum(pl.program_id(0), 0)
    n_heads = slab // HEAD_DIM

    gate_lhs = []
    for h in range(n_heads):
        km = jnp.where((lane // HEAD_DIM) == h, kmean, 0.0)
        km_hi = km.astype(BF16)
        gate_lhs += [km_hi, (km - km_hi.astype(F32)).astype(BF16)]
    g2 = jnp.dot(jnp.concatenate(gate_lhs, axis=0), qt_ref[0], preferred_element_type=F32)

    def selection_bias(h):
        g_hi = g2[2 * h * nb:(2 * h + 1) * nb]
        g_lo = g2[(2 * h + 1) * nb:(2 * h + 2) * nb]
        gate = jnp.where(past, g_hi + g_lo, -jnp.inf)
        rank = jnp.zeros((nb, S), jnp.int32)
        for other in range(nb):
            row = gate[other:other + 1, :]
            beats = jnp.where(row > gate, 1,
                              jnp.where(jnp.logical_and(row == gate, other < blk_id), 1, 0))
            rank = rank + beats
        return jnp.where(jnp.logical_and(rank < n_sel, past), 0.0, NEG_INF)

    class Stage:
        def __init__(self, index, h, t, bias):
            self.h, self.t, self.bias = h, t, bias
            self.s_scr = (s_scr0, s_scr1)[index % 2]
            self.p_scr = (p_scr0, p_scr1)[index % 2]
            self.qcols = slice(t * sq, (t + 1) * sq)
            self.n_kv = per * (t + 1)
            self.head = slice(h * HEAD_DIM, (h + 1) * HEAD_DIM)
            pieces = []
            if h > 0:
                pieces.append(jnp.zeros((h * HEAD_DIM, sq), BF16))
            pieces.append(qt_ref[0, self.head, self.qcols])
            if h < n_heads - 1:
                pieces.append(jnp.zeros(((n_heads - 1 - h) * HEAD_DIM, sq), BF16))
            self.qm = jnp.concatenate(pieces, axis=0)
            self.m_tile = None
            self.m = None

        def row_bias(self, j):
            if j == per * self.t:
                return jnp.where(second_block, self.bias[j:j + 1, self.qcols], 0.0)
            if j == per * self.t + 1:
                return None
            return self.bias[j:j + 1, self.qcols]

        def score_block(self, j):
            rows = slice(j * blk, (j + 1) * blk)
            sj = jnp.dot(k_ref[0, rows, :], self.qm, preferred_element_type=F32)
            if j == per * self.t:
                sj = sj + diag_lo[...]
            elif j == per * self.t + 1:
                sj = sj + diag_hi[...]
            self.s_scr[rows, :] = sj
            mj = jnp.max(sj.reshape(blk // V7X_SUBLANES, V7X_SUBLANES, sq), axis=0)
            rb = self.row_bias(j)
            if rb is not None:
                mj = mj + rb
            self.m_tile = mj if self.m_tile is None else jnp.maximum(self.m_tile, mj)

        def prob_block(self, j):
            rows = slice(j * blk, (j + 1) * blk)
            if self.m is None:
                self.m = jnp.max(self.m_tile, axis=0, keepdims=True)
            rb = self.row_bias(j)
            shift = self.m if rb is None else self.m - rb
            s = self.s_scr[pl.ds(pl.multiple_of(row_base + j * blk, blk), blk), :]
            self.p_scr[rows, :] = jnp.exp2(s - shift).astype(BF16)

        def weighted_values(self):
            kv = self.n_kv * blk
            vt = jnp.concatenate([vt_ref[0, self.head, 0:kv],
                                  jnp.ones((V7X_BF16_ROWS, kv), BF16)], axis=0)
            p = self.p_scr[pl.ds(pl.multiple_of(row_base, blk), kv), :]
            o = jnp.dot(vt, p, preferred_element_type=F32)
            l = o[HEAD_DIM:HEAD_DIM + 1, :]
            o_ref[0, self.head, self.qcols] = (o[0:HEAD_DIM, :] / l).astype(BF16)

    stages = []
    for h in range(n_heads):
        bias = selection_bias(h)
        for t in range(S // sq):
            stages.append((h, t, bias))
    live = {}
    for k in range(len(stages) + 2):
        if k - 2 in live:
            live.pop(k - 2).weighted_values()
        prev = live.get(k - 1)
        pending = list(range(prev.n_kv)) if prev is not None else []
        if k < len(stages):
            cur = live[k] = Stage(k, *stages[k])
            for j in range(cur.n_kv):
                cur.score_block(j)
                if pending:
                    prev.prob_block(pending.pop(0))
        for j in pending:
            prev.prob_block(j)


def _moba(qt, k, vt):
    B, _, S = qt.shape
    slab = HEADS_PER_STEP * HEAD_DIM
    blocks = 4 * _nbytes((S, slab), BF16)
    sq = MOBA_QUERY_TILE
    score_bufs = [pltpu.VMEM((S, sq), F32)] * 2 + [pltpu.VMEM((S, sq), BF16)] * 2
    diag_bufs = [pltpu.VMEM((MOBA_BLOCK, sq), F32)] * 2
    scratch = (2 * _nbytes((S, sq), F32) + 2 * _nbytes((S, sq), BF16)
               + 2 * _nbytes((MOBA_BLOCK, sq), F32))
    temps = _nbytes((S, sq), F32) + 8 * _nbytes((S // MOBA_BLOCK * 2, S), F32)
    return pl.pallas_call(
        _moba_kernel,
        name="moba",
        grid=(B, D_ATTN // slab),
        in_specs=[
            pl.BlockSpec((1, slab, S), lambda b, c: (b, c, 0)),
            pl.BlockSpec((1, S, slab), lambda b, c: (b, 0, c)),
            pl.BlockSpec((1, slab, S), lambda b, c: (b, c, 0)),
        ],
        out_specs=pl.BlockSpec((1, slab, S), lambda b, c: (b, c, 0)),
        out_shape=jax.ShapeDtypeStruct((B, D_ATTN, S), BF16),
        scratch_shapes=score_bufs + diag_bufs,
        compiler_params=pltpu.CompilerParams(
            dimension_semantics=("parallel", "parallel"),
            vmem_limit_bytes=_vmem_limit(blocks, scratch, temps)),
    )(qt, k, vt)


def _mix_out_kernel(yt_ref, yp_ref, x_ref, wo_ref, g_ref, o_ref):
    ya = yt_ref[0].T
    mix = jnp.dot(ya, wo_ref[0:D_ATTN, :], preferred_element_type=F32)
    mix = mix + jnp.dot(yp_ref[0], wo_ref[D_ATTN:, :], preferred_element_type=F32)
    o_ref[0] = x_ref[0] + mix * _rms_scale(mix) * g_ref[...]


def _mix_out(yt, yp, x, w_out, g_post):
    B, S, _ = x.shape
    ts = SEQ_TILE
    blocks = (2 * _nbytes((ts, D_ATTN), BF16) + 2 * _nbytes((ts, D_MODEL), F32)
              + _nbytes(w_out.shape, BF16))
    temps = 4 * _nbytes((ts, D_MODEL), F32)
    row = lambda b, s: (0, 0)
    return pl.pallas_call(
        _mix_out_kernel,
        name="mix_out",
        grid=(B, S // ts),
        in_specs=[
            pl.BlockSpec((1, D_ATTN, ts), lambda b, s: (b, 0, s)),
            pl.BlockSpec((1, ts, D_POOL), lambda b, s: (b, s, 0)),
            pl.BlockSpec((1, ts, D_MODEL), lambda b, s: (b, s, 0)),
            pl.BlockSpec(w_out.shape, row),
            pl.BlockSpec((1, D_MODEL), row),
        ],
        out_specs=pl.BlockSpec((1, ts, D_MODEL), lambda b, s: (b, s, 0)),
        out_shape=jax.ShapeDtypeStruct((B, S, D_MODEL), F32),
        compiler_params=pltpu.CompilerParams(
            dimension_semantics=("parallel", "parallel"),
            vmem_limit_bytes=_vmem_limit(blocks, 0, temps)),
    )(yt, yp, x, w_out, g_post)


def _conv_ffn_kernel(x_ref, gpre_ref, wup_ref, cw_ref, cb_ref, wd_ref, gpost_ref, o_ref,
                     h_scr, abuf0, abuf1, carry, z_scr):
    r = pl.program_id(1)
    rows = x_ref.shape[1]
    tf = FF_CHUNK
    n_chunks = D_FF // tf

    @pl.when(r == 0)
    def _():
        carry[...] = jnp.zeros_like(carry)

    x1 = x_ref[0]
    h_scr[...] = (x1 * _rms_scale(x1) * gpre_ref[...]).astype(BF16)

    for c in range(n_chunks):
        abuf = (abuf0, abuf1)[c % 2]
        gate_cols = slice(c * tf, (c + 1) * tf)
        val_cols = slice(D_FF + c * tf, D_FF + (c + 1) * tf)

        def pair(ref, rows_=slice(None)):
            return jnp.concatenate([ref[rows_, gate_cols], ref[rows_, val_cols]], axis=1)

        a = jnp.dot(h_scr[...], pair(wup_ref), preferred_element_type=F32)
        abuf[0:CONV_HALO, :] = carry[c]
        abuf[CONV_HALO:CONV_HALO + rows, :] = a
        carry[c] = a[rows - CONV_HALO:rows, :]
        conv = a * pair(cw_ref, slice(CONV_WIDTH - 1, CONV_WIDTH)) + pair(cb_ref)
        for back in range(1, CONV_WIDTH):
            tap = CONV_WIDTH - 1 - back
            conv = conv + abuf[pl.ds(CONV_HALO - back, rows), :] * pair(cw_ref, slice(tap, tap + 1))
        gate = conv[:, 0:tf]
        val = conv[:, tf:2 * tf]
        z_scr[:, c * tf:(c + 1) * tf] = (gate * (1.0 / (1.0 + jnp.exp(-gate))) * val).astype(BF16)

    f = jnp.dot(z_scr[...], wd_ref[...], preferred_element_type=F32)
    o_ref[0] = x_ref[0] + f * _rms_scale(f) * gpost_ref[...]


def _conv_ffn(x1, g_pre, wup, cw, cb, wd, g_post):
    B, S, _ = x1.shape
    rows = SEQ_TILE
    tf = FF_CHUNK
    n_chunks = D_FF // tf
    blocks = 2 * _nbytes((rows, D_MODEL), F32) + _nbytes((V7X_SUBLANES + 1, 2 * D_FF), F32)
    resident = _nbytes(wup.shape, BF16) + _nbytes(wd.shape, BF16)
    scratch = (_nbytes((rows, D_MODEL), BF16) + 2 * _nbytes((CONV_HALO + rows, 2 * tf), F32)
               + _nbytes((n_chunks, CONV_HALO, 2 * tf), F32) + _nbytes((rows, D_FF), BF16))
    temps = 6 * _nbytes((rows, 2 * tf), F32) + _nbytes((rows, D_MODEL), F32)
    row = lambda b, r: (0, 0)
    once = pl.Buffered(1)
    return pl.pallas_call(
        _conv_ffn_kernel,
        name="conv_ffn",
        grid=(B, S // rows),
        in_specs=[
            pl.BlockSpec((1, rows, D_MODEL), lambda b, r: (b, r, 0)),
            pl.BlockSpec((1, D_MODEL), row),
            pl.BlockSpec(wup.shape, row, pipeline_mode=once),
            pl.BlockSpec(cw.shape, row),
            pl.BlockSpec(cb.shape, row),
            pl.BlockSpec(wd.shape, row, pipeline_mode=once),
            pl.BlockSpec((1, D_MODEL), row),
        ],
        out_specs=pl.BlockSpec((1, rows, D_MODEL), lambda b, r: (b, r, 0)),
        out_shape=jax.ShapeDtypeStruct((B, S, D_MODEL), F32),
        scratch_shapes=[
            pltpu.VMEM((rows, D_MODEL), BF16),
            pltpu.VMEM((CONV_HALO + rows, 2 * tf), F32),
            pltpu.VMEM((CONV_HALO + rows, 2 * tf), F32),
            pltpu.VMEM((n_chunks, CONV_HALO, 2 * tf), F32),
            pltpu.VMEM((rows, D_FF), BF16),
        ],
        compiler_params=pltpu.CompilerParams(
            dimension_semantics=("parallel", "arbitrary"),
            vmem_limit_bytes=_vmem_limit(blocks, scratch + resident, temps)),
    )(x1, g_pre, wup, cw, cb, wd, g_post)


def _rope_tables(seq_len):
    inv_freq = ROPE_THETA ** (-jnp.arange(0, HEAD_DIM, 2, dtype=F32) / HEAD_DIM)
    ang = jnp.arange(seq_len, dtype=F32)[:, None] * inv_freq[None, :]
    cos, sin = jnp.cos(ang), jnp.sin(ang)
    reps = V7X_LANES // HEAD_DIM
    cos_t = jnp.tile(jnp.concatenate([cos, cos], axis=1), (1, reps))
    sin_t = jnp.tile(jnp.concatenate([-sin, sin], axis=1), (1, reps))
    return cos_t, sin_t


def kernel(x, norm_mix_pre, w_in, w_pool, b_pool, pool_scale, w_out, norm_mix_post,
           norm_ffn_pre, w_up, conv_w, conv_b, w_down, norm_ffn_post):
    B, S, D = x.shape
    assert D == D_MODEL and S % SEQ_TILE == 0 and S % MOBA_BLOCK == 0 and D_FF % FF_CHUNK == 0
    depth = w_in.shape[0]
    cos_t, sin_t = _rope_tables(S)
    for l in range(depth):
        qt, k, vt, yp = _mix_in(
            x, norm_mix_pre[l][None, :], w_in[l].astype(BF16), cos_t, sin_t,
            w_pool[l].astype(BF16), b_pool[l][None, :], pool_scale[l][None, :])
        yt = _moba(qt, k, vt)
        x1 = _mix_out(yt, yp, x, w_out[l].astype(BF16), norm_mix_post[l][None, :])
        x = _conv_ffn(
            x1, norm_ffn_pre[l][None, :], w_up[l].astype(BF16),
            conv_w[l].reshape(CONV_WIDTH, 2 * D_FF), conv_b[l][None, :],
            w_down[l].astype(BF16), norm_ffn_post[l][None, :])
    return x
```

```python
import math

import jax
import jax.numpy as jnp
from jax import lax
from jax.experimental import pallas as pl
from jax.experimental.pallas import tpu as pltpu

D_MODEL = 1024
N_ATTN_HEADS = 8
HEAD_DIM = 64
D_ATTN = N_ATTN_HEADS * HEAD_DIM
D_POOL = D_MODEL - D_ATTN
POOL_WINDOWS = (2, 4, 8, 16)
POOL_GROUP_DIM = D_POOL // len(POOL_WINDOWS)
D_IN_PROJ = 3 * D_ATTN + D_POOL
MOBA_BLOCK = 256
MOBA_TOPK = 3
ROPE_THETA = 10000.0
D_FF = ((8 * D_MODEL // 3 + 127) // 128) * 128
CONV_WIDTH = 3
RMS_EPS = 1e-6
NEG_INF = -1e30
LOG2_E = 1.4426950408889634

V7X_LANES = 128
V7X_SUBLANES = 8
V7X_BF16_ROWS = 16
V7X_MXU_DIM = 256
V7X_VMEM_BYTES = 64 * 1024 * 1024

SEQ_TILE = 512
TILES_PER_STEP = 2
HEADS_PER_STEP = V7X_MXU_DIM // HEAD_DIM
V7X_NUM_MXU = 2
MOBA_QUERY_TILE = V7X_NUM_MXU * MOBA_BLOCK
MOBA_SUB_ROWS = 64
MOBA_VALUE_LAG = 2
FF_CHUNK = 256
POOL_HALO = max(POOL_WINDOWS)
POOL_PAD = V7X_SUBLANES
CONV_HALO = V7X_SUBLANES

F32 = jnp.float32
BF16 = jnp.bfloat16


def _vmem_limit(block_bytes, scratch_bytes, temp_bytes):
    need = 2 * block_bytes + scratch_bytes + temp_bytes
    assert need <= V7X_VMEM_BYTES, need
    return int(need)


def _nbytes(shape, dtype):
    return math.prod(shape) * jnp.dtype(dtype).itemsize


def _rms_scale(v):
    return lax.rsqrt(jnp.mean(v * v, axis=-1, keepdims=True) + RMS_EPS)


def _mix_in_kernel(x_ref, g_ref, w32_ref, cos_ref, sin_ref, wp_ref, bp_ref, ps_ref,
                   wout32_ref, wup32_ref, wd32_ref,
                   qt_ref, k_ref, vt_ref, yp_ref, wout_ref, wup_ref, wd_ref,
                   w_ref, ubuf, lvl1, lvl2, lvl3):
    s = pl.program_id(1)
    ts = SEQ_TILE
    top = POOL_PAD + POOL_HALO
    gd = POOL_GROUP_DIM

    @pl.when(jnp.logical_and(pl.program_id(0) == 0, s == 0))
    def _():
        w_ref[...] = w32_ref[...].astype(BF16)

    @pl.when(s == 0)
    def _():
        ubuf[ts + POOL_PAD:ts + top, :] = jnp.zeros((POOL_HALO, D_POOL), F32)

    wout_ref[...] = wout32_ref[...].astype(BF16)
    wup_ref[...] = wup32_ref[...].astype(BF16)
    wd_ref[...] = wd32_ref[...].astype(BF16)

    ubuf[0:POOL_PAD, :] = jnp.zeros((POOL_PAD, D_POOL), F32)
    lvl1[0:POOL_PAD, :] = jnp.zeros((POOL_PAD, D_POOL), F32)
    lvl2[0:POOL_PAD, :] = jnp.zeros((POOL_PAD, D_POOL - gd), F32)
    for r0 in range(0, x_ref.shape[1], ts):
        _mix_in_tile(slice(r0, r0 + ts), s * x_ref.shape[1] + r0, x_ref, g_ref, w_ref, cos_ref,
                     sin_ref, wp_ref, bp_ref, ps_ref, qt_ref, k_ref, vt_ref, yp_ref,
                     ubuf, lvl1, lvl2, lvl3)


def _mix_in_tile(rows, t0, x_ref, g_ref, w_ref, cos_ref, sin_ref, wp_ref, bp_ref, ps_ref,
                 qt_ref, k_ref, vt_ref, yp_ref, ubuf, lvl1, lvl2, lvl3):
    ts = SEQ_TILE
    top = POOL_PAD + POOL_HALO
    gd = POOL_GROUP_DIM
    xf = x_ref[0, rows, :]
    hn = (xf * _rms_scale(xf) * g_ref[...]).astype(BF16)

    u = jnp.dot(hn, w_ref[:, 3 * D_ATTN:], preferred_element_type=F32)
    ubuf[POOL_PAD:top, :] = ubuf[ts + POOL_PAD:ts + top, :]
    ubuf[top:top + ts, :] = u
    ext = ts + POOL_HALO
    s2 = ubuf[POOL_PAD:POOL_PAD + ext, :] + ubuf[pl.ds(POOL_PAD - 1, ext), :]
    lvl1[POOL_PAD:POOL_PAD + ext, :] = s2
    s4 = s2[:, gd:] + lvl1[pl.ds(POOL_PAD - 2, ext), gd:]
    lvl2[POOL_PAD:POOL_PAD + ext, :] = s4
    s8 = s4[:, gd:] + lvl2[pl.ds(POOL_PAD - 4, ext), gd:]
    lvl3[POOL_PAD:POOL_PAD + ext, :] = s8
    s16 = s8[POOL_HALO:, gd:] + lvl3[top - 8:top - 8 + ts, gd:]
    sums = (s2[POOL_HALO:, 0:gd], s4[POOL_HALO:, 0:gd], s8[POOL_HALO:, 0:gd], s16)
    t_pos = t0 + lax.broadcasted_iota(jnp.int32, (ts, gd), 0)
    for g, w in enumerate(POOL_WINDOWS):
        cols = slice(g * gd, (g + 1) * gd)
        cnt = jnp.minimum(t_pos + 1, w).astype(F32)
        diff = (sums[g] / cnt - u[:, cols]).astype(BF16)
        yg = jnp.dot(diff, wp_ref[g].astype(BF16), preferred_element_type=F32)
        yg = (yg + bp_ref[:, cols]) * ps_ref[:, cols]
        yp_ref[0, rows, cols] = yg.astype(BF16)

    reps = D_ATTN // cos_ref.shape[1]
    cos = jnp.concatenate([cos_ref[rows, :]] * reps, axis=1)
    sin = jnp.concatenate([sin_ref[rows, :]] * reps, axis=1)
    lane = lax.broadcasted_iota(jnp.int32, (ts, D_ATTN), 1)
    first_half = (lane % HEAD_DIM) < (HEAD_DIM // 2)

    def rope(t):
        partner = jnp.where(first_half,
                            pltpu.roll(t, D_ATTN - HEAD_DIM // 2, axis=1),
                            pltpu.roll(t, HEAD_DIM // 2, axis=1))
        return t * cos + partner * sin

    qf = jnp.dot(hn, w_ref[:, 0:D_ATTN], preferred_element_type=F32)
    qt_ref[0, :, rows] = (rope(qf) * (HEAD_DIM ** -0.5 * LOG2_E)).T.astype(BF16)
    kf = jnp.dot(hn, w_ref[:, D_ATTN:2 * D_ATTN], preferred_element_type=F32)
    k_ref[0, rows, :] = rope(kf).astype(BF16)
    vf = jnp.dot(hn, w_ref[:, 2 * D_ATTN:3 * D_ATTN], preferred_element_type=F32)
    vt_ref[0, :, rows] = vf.T.astype(BF16)


def _mix_in(x, g_pre, w_in, cos_t, sin_t, w_pool, b_pool, pool_scale, later_weights):
    B, S, _ = x.shape
    ts = TILES_PER_STEP * SEQ_TILE
    n_s = S // ts
    n_steps = B * n_s
    lvl_rows = POOL_PAD + POOL_HALO + SEQ_TILE
    gd = POOL_GROUP_DIM
    slabs = []
    for w in later_weights:
        assert w.shape[0] % (n_steps * V7X_BF16_ROWS) == 0, w.shape
        slabs.append((w.shape[0] // n_steps, w.shape[1]))
    blocks = (_nbytes((ts, D_MODEL), F32) + 2 * _nbytes((ts, V7X_LANES), F32)
              + _nbytes(w_pool.shape, F32) + 4 * _nbytes((ts, D_ATTN), BF16)
              + sum(_nbytes(sl, F32) + _nbytes(sl, BF16) for sl in slabs))
    scratch = (_nbytes((lvl_rows, 4 * D_POOL - 3 * gd), F32) + _nbytes(w_in.shape, F32)
               + _nbytes(w_in.shape, BF16))
    temps = 12 * _nbytes((SEQ_TILE, D_ATTN), F32)
    row = lambda b, s: (0, 0)
    step = lambda b, s: (b * n_s + s, 0)
    return pl.pallas_call(
        _mix_in_kernel,
        name="mix_in",
        grid=(B, n_s),
        in_specs=[
            pl.BlockSpec((1, ts, D_MODEL), lambda b, s: (b, s, 0)),
            pl.BlockSpec((1, D_MODEL), row),
            pl.BlockSpec(w_in.shape, row, pipeline_mode=pl.Buffered(1)),
            pl.BlockSpec((ts, V7X_LANES), lambda b, s: (s, 0)),
            pl.BlockSpec((ts, V7X_LANES), lambda b, s: (s, 0)),
            pl.BlockSpec(w_pool.shape, lambda b, s: (0, 0, 0)),
            pl.BlockSpec((1, D_POOL), row),
            pl.BlockSpec((1, D_POOL), row),
        ] + [pl.BlockSpec(sl, step) for sl in slabs],
        out_specs=[
            pl.BlockSpec((1, D_ATTN, ts), lambda b, s: (b, 0, s)),
            pl.BlockSpec((1, ts, D_ATTN), lambda b, s: (b, s, 0)),
            pl.BlockSpec((1, D_ATTN, ts), lambda b, s: (b, 0, s)),
            pl.BlockSpec((1, ts, D_POOL), lambda b, s: (b, s, 0)),
        ] + [pl.BlockSpec(sl, step) for sl in slabs],
        out_shape=[
            jax.ShapeDtypeStruct((B, D_ATTN, S), BF16),
            jax.ShapeDtypeStruct((B, S, D_ATTN), BF16),
            jax.ShapeDtypeStruct((B, D_ATTN, S), BF16),
            jax.ShapeDtypeStruct((B, S, D_POOL), BF16),
        ] + [jax.ShapeDtypeStruct(w.shape, BF16) for w in later_weights],
        scratch_shapes=[
            pltpu.VMEM(w_in.shape, BF16),
            pltpu.VMEM((lvl_rows, D_POOL), F32),
            pltpu.VMEM((lvl_rows, D_POOL), F32),
            pltpu.VMEM((lvl_rows, D_POOL - gd), F32),
            pltpu.VMEM((lvl_rows, D_POOL - 2 * gd), F32),
        ],
        compiler_params=pltpu.CompilerParams(
            dimension_semantics=("arbitrary", "arbitrary"),
            vmem_limit_bytes=_vmem_limit(blocks, scratch, temps)),
    )(x, g_pre, w_in, cos_t, sin_t, w_pool, b_pool, pool_scale, *later_weights)


def _moba_kernel(qt_ref, k_ref, vt_ref, o_ref, s_scr0, s_scr1, p_scr0, p_scr1,
                 diag_lo, diag_hi):
    slab = qt_ref.shape[1]
    S = qt_ref.shape[2]
    blk = MOBA_BLOCK
    nb = S // blk
    n_sel = min(MOBA_TOPK, nb - 1)
    sq = MOBA_QUERY_TILE
    per = sq // blk

    lane = lax.broadcasted_iota(jnp.int32, (1, slab), 1)
    kmean = jnp.concatenate(
        [jnp.sum(k_ref[0, j * blk:(j + 1) * blk, :].astype(F32), axis=0, keepdims=True)
         for j in range(nb)], axis=0) * (1.0 / blk)

    blk_id = lax.broadcasted_iota(jnp.int32, (nb, S), 0)
    q_blk = lax.broadcasted_iota(jnp.int32, (nb, S), 1) // blk
    past = blk_id < q_blk
    key_row = lax.broadcasted_iota(jnp.int32, (blk, sq), 0)
    qry_col = lax.broadcasted_iota(jnp.int32, (blk, sq), 1)
    diag_lo[...] = jnp.where(key_row <= qry_col, 0.0, NEG_INF)
    diag_hi[...] = jnp.where(key_row + blk <= qry_col, 0.0, NEG_INF)
    second_block = lax.broadcasted_iota(jnp.int32, (1, sq), 1) >= blk
    row_base = jnp.minimum(pl.program_id(0), 0)
    n_heads = slab // HEAD_DIM

    gate_lhs = []
    for h in range(n_heads):
        km = jnp.where((lane // HEAD_DIM) == h, kmean, 0.0)
        km_hi = km.astype(BF16)
        gate_lhs += [km_hi, (km - km_hi.astype(F32)).astype(BF16)]
    g2 = jnp.dot(jnp.concatenate(gate_lhs, axis=0), qt_ref[0], preferred_element_type=F32)

    def selection_bias(h):
        g_hi = g2[2 * h * nb:(2 * h + 1) * nb]
        g_lo = g2[(2 * h + 1) * nb:(2 * h + 2) * nb]
        gate = jnp.where(past, g_hi + g_lo, -jnp.inf)
        rank = jnp.zeros((nb, S), jnp.int32)
        for other in range(nb):
            row = gate[other:other + 1, :]
            beats = jnp.where(row > gate, 1,
                              jnp.where(jnp.logical_and(row == gate, other < blk_id), 1, 0))
            rank = rank + beats
        return jnp.where(jnp.logical_and(rank < n_sel, past), 0.0, NEG_INF)

    class Stage:
        def __init__(self, index, h, t, bias):
            self.h, self.t, self.bias = h, t, bias
            self.s_scr = (s_scr0, s_scr1)[index % 2]
            self.p_scr = (p_scr0, p_scr1)[index % MOBA_VALUE_LAG]
            self.qcols = slice(t * sq, (t + 1) * sq)
            self.n_kv = per * (t + 1)
            self.head = slice(h * HEAD_DIM, (h + 1) * HEAD_DIM)
            pieces = []
            if h > 0:
                pieces.append(jnp.zeros((h * HEAD_DIM, sq), BF16))
            pieces.append(qt_ref[0, self.head, self.qcols])
            if h < n_heads - 1:
                pieces.append(jnp.zeros(((n_heads - 1 - h) * HEAD_DIM, sq), BF16))
            self.qm = jnp.concatenate(pieces, axis=0)
            self.m_tile = None
            self.m = None

        def row_bias(self, j):
            if j == per * self.t:
                return jnp.where(second_block, self.bias[j:j + 1, self.qcols], 0.0)
            if j == per * self.t + 1:
                return None
            return self.bias[j:j + 1, self.qcols]

        def score_matmul(self, j):
            return jnp.dot(k_ref[0, j * blk:(j + 1) * blk, :], self.qm,
                           preferred_element_type=F32)

        def score_rows(self, j, sj, part):
            sub = slice(part * MOBA_SUB_ROWS, (part + 1) * MOBA_SUB_ROWS)
            s = sj[sub, :]
            if j == per * self.t:
                s = s + diag_lo[sub, :]
            elif j == per * self.t + 1:
                s = s + diag_hi[sub, :]
            r0 = j * blk + part * MOBA_SUB_ROWS
            self.s_scr[r0:r0 + MOBA_SUB_ROWS, :] = s
            mj = jnp.max(s.reshape(MOBA_SUB_ROWS // V7X_SUBLANES, V7X_SUBLANES, sq), axis=0)
            rb = self.row_bias(j)
            if rb is not None:
                mj = mj + rb
            self.m_tile = mj if self.m_tile is None else jnp.maximum(self.m_tile, mj)

        def prob_rows(self, j, part):
            if self.m is None:
                self.m = jnp.max(self.m_tile, axis=0, keepdims=True)
            rb = self.row_bias(j)
            shift = self.m if rb is None else self.m - rb
            r0 = j * blk + part * MOBA_SUB_ROWS
            s = self.s_scr[pl.ds(pl.multiple_of(row_base + r0, MOBA_SUB_ROWS), MOBA_SUB_ROWS), :]
            self.p_scr[r0:r0 + MOBA_SUB_ROWS, :] = jnp.exp2(s - shift).astype(BF16)

        def weighted_values(self):
            kv = self.n_kv * blk
            vt = jnp.concatenate([vt_ref[0, self.head, 0:kv],
                                  jnp.ones((V7X_BF16_ROWS, kv), BF16)], axis=0)
            p = self.p_scr[pl.ds(pl.multiple_of(row_base, blk), kv), :]
            o = jnp.dot(vt, p, preferred_element_type=F32)
            l = o[HEAD_DIM:HEAD_DIM + 1, :]
            o_ref[0, self.head, self.qcols] = (o[0:HEAD_DIM, :] / l).astype(BF16)

    stages = []
    for h in range(n_heads):
        bias = selection_bias(h)
        for t in range(S // sq):
            stages.append((h, t, bias))
    parts = blk // MOBA_SUB_ROWS
    live = {}
    for k in range(len(stages) + MOBA_VALUE_LAG):
        if k - MOBA_VALUE_LAG in live:
            live.pop(k - MOBA_VALUE_LAG).weighted_values()
        prev = live.get(k - 1)
        pending = ([(j, part) for j in range(prev.n_kv) for part in range(parts)]
                   if prev is not None else [])
        if k < len(stages):
            cur = live[k] = Stage(k, *stages[k])
            for j in range(cur.n_kv):
                sj = cur.score_matmul(j)
                for part in range(parts):
                    cur.score_rows(j, sj, part)
                    if pending:
                        prev.prob_rows(*pending.pop(0))
        for j, part in pending:
            prev.prob_rows(j, part)


def _moba(qt, k, vt):
    B, _, S = qt.shape
    slab = HEADS_PER_STEP * HEAD_DIM
    blocks = 4 * _nbytes((S, slab), BF16)
    sq = MOBA_QUERY_TILE
    score_bufs = [pltpu.VMEM((S, sq), F32)] * 2 + [pltpu.VMEM((S, sq), BF16)] * MOBA_VALUE_LAG
    diag_bufs = [pltpu.VMEM((MOBA_BLOCK, sq), F32)] * 2
    scratch = (2 * _nbytes((S, sq), F32) + MOBA_VALUE_LAG * _nbytes((S, sq), BF16)
               + 2 * _nbytes((MOBA_BLOCK, sq), F32))
    temps = _nbytes((S, sq), F32) + 8 * _nbytes((S // MOBA_BLOCK * 2, S), F32)
    return pl.pallas_call(
        _moba_kernel,
        name="moba",
        grid=(B, D_ATTN // slab),
        in_specs=[
            pl.BlockSpec((1, slab, S), lambda b, c: (b, c, 0)),
            pl.BlockSpec((1, S, slab), lambda b, c: (b, 0, c)),
            pl.BlockSpec((1, slab, S), lambda b, c: (b, c, 0)),
        ],
        out_specs=pl.BlockSpec((1, slab, S), lambda b, c: (b, c, 0)),
        out_shape=jax.ShapeDtypeStruct((B, D_ATTN, S), BF16),
        scratch_shapes=score_bufs + diag_bufs,
        compiler_params=pltpu.CompilerParams(
            dimension_semantics=("parallel", "parallel"),
            vmem_limit_bytes=_vmem_limit(blocks, scratch, temps)),
    )(qt, k, vt)


def _mix_out_kernel(yt_ref, yp_ref, x_ref, wo_ref, g_ref, o_ref):
    ya = yt_ref[0].T
    mix = jnp.dot(ya, wo_ref[0:D_ATTN, :], preferred_element_type=F32)
    mix = mix + jnp.dot(yp_ref[0], wo_ref[D_ATTN:, :], preferred_element_type=F32)
    o_ref[0] = x_ref[0] + mix * _rms_scale(mix) * g_ref[...]


def _mix_out(yt, yp, x, w_out, g_post):
    B, S, _ = x.shape
    ts = SEQ_TILE
    blocks = (2 * _nbytes((ts, D_ATTN), BF16) + 2 * _nbytes((ts, D_MODEL), F32)
              + _nbytes(w_out.shape, BF16))
    temps = 4 * _nbytes((ts, D_MODEL), F32)
    row = lambda b, s: (0, 0)
    return pl.pallas_call(
        _mix_out_kernel,
        name="mix_out",
        grid=(B, S // ts),
        in_specs=[
            pl.BlockSpec((1, D_ATTN, ts), lambda b, s: (b, 0, s)),
            pl.BlockSpec((1, ts, D_POOL), lambda b, s: (b, s, 0)),
            pl.BlockSpec((1, ts, D_MODEL), lambda b, s: (b, s, 0)),
            pl.BlockSpec(w_out.shape, row),
            pl.BlockSpec((1, D_MODEL), row),
        ],
        out_specs=pl.BlockSpec((1, ts, D_MODEL), lambda b, s: (b, s, 0)),
        out_shape=jax.ShapeDtypeStruct((B, S, D_MODEL), F32),
        compiler_params=pltpu.CompilerParams(
            dimension_semantics=("parallel", "parallel"),
            vmem_limit_bytes=_vmem_limit(blocks, 0, temps)),
    )(yt, yp, x, w_out, g_post)


def _conv_ffn_kernel(x_ref, gpre_ref, wup_ref, cw_ref, cb_ref, wd_ref, gpost_ref, o_ref,
                     h_scr, abuf0, abuf1, carry, z_scr):
    r = pl.program_id(1)
    rows = x_ref.shape[1]
    tf = FF_CHUNK
    n_chunks = D_FF // tf

    @pl.when(r == 0)
    def _():
        carry[...] = jnp.zeros_like(carry)

    x1 = x_ref[0]
    h_scr[...] = (x1 * _rms_scale(x1) * gpre_ref[...]).astype(BF16)
    half_gate = jnp.where(lax.broadcasted_iota(jnp.int32, (1, 2 * tf), 1) < tf, 0.5, 1.0)

    for c in range(n_chunks):
        abuf = (abuf0, abuf1)[c % 2]
        gate_cols = slice(c * tf, (c + 1) * tf)
        val_cols = slice(D_FF + c * tf, D_FF + (c + 1) * tf)

        def pair(ref, rows_=slice(None)):
            return jnp.concatenate([ref[rows_, gate_cols], ref[rows_, val_cols]], axis=1)

        a = jnp.dot(h_scr[...], pair(wup_ref), preferred_element_type=F32)
        abuf[0:CONV_HALO, :] = carry[c]
        abuf[CONV_HALO:CONV_HALO + rows, :] = a
        carry[c] = a[rows - CONV_HALO:rows, :]
        conv = a * (pair(cw_ref, slice(CONV_WIDTH - 1, CONV_WIDTH)) * half_gate) + pair(cb_ref) * half_gate
        for back in range(1, CONV_WIDTH):
            tap = CONV_WIDTH - 1 - back
            conv = conv + (abuf[pl.ds(CONV_HALO - back, rows), :]
                           * (pair(cw_ref, slice(tap, tap + 1)) * half_gate))
        gate_half = conv[:, 0:tf]
        val = conv[:, tf:2 * tf]
        z_scr[:, c * tf:(c + 1) * tf] = (
            (gate_half + gate_half * jnp.tanh(gate_half)) * val).astype(BF16)

    f = jnp.dot(z_scr[...], wd_ref[...], preferred_element_type=F32)
    o_ref[0] = x_ref[0] + f * _rms_scale(f) * gpost_ref[...]


def _conv_ffn(x1, g_pre, wup, cw, cb, wd, g_post):
    B, S, _ = x1.shape
    rows = SEQ_TILE
    tf = FF_CHUNK
    n_chunks = D_FF // tf
    blocks = 2 * _nbytes((rows, D_MODEL), F32) + _nbytes((V7X_SUBLANES + 1, 2 * D_FF), F32)
    resident = _nbytes(wup.shape, BF16) + _nbytes(wd.shape, BF16)
    scratch = (_nbytes((rows, D_MODEL), BF16) + 2 * _nbytes((CONV_HALO + rows, 2 * tf), F32)
               + _nbytes((n_chunks, CONV_HALO, 2 * tf), F32) + _nbytes((rows, D_FF), BF16))
    temps = 6 * _nbytes((rows, 2 * tf), F32) + _nbytes((rows, D_MODEL), F32)
    row = lambda b, r: (0, 0)
    once = pl.Buffered(1)
    return pl.pallas_call(
        _conv_ffn_kernel,
        name="conv_ffn",
        grid=(B, S // rows),
        in_specs=[
            pl.BlockSpec((1, rows, D_MODEL), lambda b, r: (b, r, 0)),
            pl.BlockSpec((1, D_MODEL), row),
            pl.BlockSpec(wup.shape, row, pipeline_mode=once),
            pl.BlockSpec(cw.shape, row),
            pl.BlockSpec(cb.shape, row),
            pl.BlockSpec(wd.shape, row, pipeline_mode=once),
            pl.BlockSpec((1, D_MODEL), row),
        ],
        out_specs=pl.BlockSpec((1, rows, D_MODEL), lambda b, r: (b, r, 0)),
        out_shape=jax.ShapeDtypeStruct((B, S, D_MODEL), F32),
        scratch_shapes=[
            pltpu.VMEM((rows, D_MODEL), BF16),
            pltpu.VMEM((CONV_HALO + rows, 2 * tf), F32),
            pltpu.VMEM((CONV_HALO + rows, 2 * tf), F32),
            pltpu.VMEM((n_chunks, CONV_HALO, 2 * tf), F32),
            pltpu.VMEM((rows, D_FF), BF16),
        ],
        compiler_params=pltpu.CompilerParams(
            dimension_semantics=("parallel", "arbitrary"),
            vmem_limit_bytes=_vmem_limit(blocks, scratch + resident, temps)),
    )(x1, g_pre, wup, cw, cb, wd, g_post)


def _rope_tables(seq_len):
    inv_freq = ROPE_THETA ** (-jnp.arange(0, HEAD_DIM, 2, dtype=F32) / HEAD_DIM)
    ang = jnp.arange(seq_len, dtype=F32)[:, None] * inv_freq[None, :]
    cos, sin = jnp.cos(ang), jnp.sin(ang)
    reps = V7X_LANES // HEAD_DIM
    cos_t = jnp.tile(jnp.concatenate([cos, cos], axis=1), (1, reps))
    sin_t = jnp.tile(jnp.concatenate([-sin, sin], axis=1), (1, reps))
    return cos_t, sin_t


def kernel(x, norm_mix_pre, w_in, w_pool, b_pool, pool_scale, w_out, norm_mix_post,
           norm_ffn_pre, w_up, conv_w, conv_b, w_down, norm_ffn_post):
    B, S, D = x.shape
    assert D == D_MODEL and S % (TILES_PER_STEP * SEQ_TILE) == 0 and D_FF % FF_CHUNK == 0
    assert S % MOBA_QUERY_TILE == 0 and MOBA_QUERY_TILE == 2 * MOBA_BLOCK
    depth = w_in.shape[0]
    cos_t, sin_t = _rope_tables(S)
    for l in range(depth):
        qt, k, vt, yp, w_out_bf, w_up_bf, w_down_bf = _mix_in(
            x, norm_mix_pre[l][None, :], w_in[l], cos_t, sin_t,
            w_pool[l], b_pool[l][None, :], pool_scale[l][None, :],
            (w_out[l], w_up[l], w_down[l]))
        yt = _moba(qt, k, vt)
        x1 = _mix_out(yt, yp, x, w_out_bf, norm_mix_post[l][None, :])
        x = _conv_ffn(
            x1, norm_ffn_pre[l][None, :], w_up_bf,
            conv_w[l].reshape(CONV_WIDTH, 2 * D_FF), conv_b[l][None, :],
            w_down_bf, norm_ffn_post[l][None, :])
    return x
```

```python
import functools
import math

import jax
import jax.numpy as jnp
from jax import lax
from jax.experimental import pallas as pl
from jax.experimental.pallas import tpu as pltpu

D_MODEL = 1024
N_ATTN_HEADS = 8
HEAD_DIM = 64
D_ATTN = N_ATTN_HEADS * HEAD_DIM
D_POOL = D_MODEL - D_ATTN
POOL_WINDOWS = (2, 4, 8, 16)
POOL_GROUP_DIM = D_POOL // len(POOL_WINDOWS)
D_IN_PROJ = 3 * D_ATTN + D_POOL
MOBA_BLOCK = 256
MOBA_TOPK = 3
ROPE_THETA = 10000.0
D_FF = ((8 * D_MODEL // 3 + 127) // 128) * 128
CONV_WIDTH = 3
RMS_EPS = 1e-6
NEG_INF = -1e30
LOG2_E = 1.4426950408889634

V7X_LANES = 128
V7X_SUBLANES = 8
V7X_BF16_ROWS = 16
V7X_MXU_DIM = 256
V7X_VMEM_BYTES = 64 * 1024 * 1024

SEQ_TILE = 512
TILES_PER_STEP = 2
HEADS_PER_STEP = V7X_MXU_DIM // HEAD_DIM
V7X_NUM_MXU = 2
MOBA_QUERY_TILE = V7X_NUM_MXU * MOBA_BLOCK
MOBA_SUB_ROWS = 64
MOBA_VALUE_LAG = 2
FF_CHUNK = 256
POOL_HALO = max(POOL_WINDOWS)
POOL_PAD = V7X_SUBLANES
CONV_HALO = V7X_SUBLANES

F32 = jnp.float32
BF16 = jnp.bfloat16


def _vmem_limit(block_bytes, scratch_bytes, temp_bytes):
    need = 2 * block_bytes + scratch_bytes + temp_bytes
    assert need <= V7X_VMEM_BYTES, need
    return int(need)


def _nbytes(shape, dtype):
    return math.prod(shape) * jnp.dtype(dtype).itemsize


def _rms_scale(v):
    return lax.rsqrt(jnp.mean(v * v, axis=-1, keepdims=True) + RMS_EPS)


def _mix_in_kernel(x_ref, g_ref, w32_ref, cos_ref, sin_ref, wp_ref, bp_ref, ps_ref,
                   wout32_ref, wup32_ref, wd32_ref,
                   qt_ref, k_ref, vt_ref, yp_ref, wout_ref, wup_ref, wd_ref,
                   w_ref, ubuf, lvl1, lvl2, lvl3):
    s = pl.program_id(1)
    ts = SEQ_TILE
    top = POOL_PAD + POOL_HALO
    gd = POOL_GROUP_DIM

    @pl.when(jnp.logical_and(pl.program_id(0) == 0, s == 0))
    def _():
        w_ref[...] = w32_ref[...].astype(BF16)

    @pl.when(s == 0)
    def _():
        ubuf[ts + POOL_PAD:ts + top, :] = jnp.zeros((POOL_HALO, D_POOL), F32)

    wout_ref[...] = wout32_ref[...].astype(BF16)
    wup_ref[...] = wup32_ref[...].astype(BF16)
    wd_ref[...] = wd32_ref[...].astype(BF16)

    ubuf[0:POOL_PAD, :] = jnp.zeros((POOL_PAD, D_POOL), F32)
    lvl1[0:POOL_PAD, :] = jnp.zeros((POOL_PAD, D_POOL), F32)
    lvl2[0:POOL_PAD, :] = jnp.zeros((POOL_PAD, D_POOL - gd), F32)
    for r0 in range(0, x_ref.shape[1], ts):
        _mix_in_tile(slice(r0, r0 + ts), s * x_ref.shape[1] + r0, x_ref, g_ref, w_ref, cos_ref,
                     sin_ref, wp_ref, bp_ref, ps_ref, qt_ref, k_ref, vt_ref, yp_ref,
                     ubuf, lvl1, lvl2, lvl3)


def _mix_in_tile(rows, t0, x_ref, g_ref, w_ref, cos_ref, sin_ref, wp_ref, bp_ref, ps_ref,
                 qt_ref, k_ref, vt_ref, yp_ref, ubuf, lvl1, lvl2, lvl3):
    ts = SEQ_TILE
    top = POOL_PAD + POOL_HALO
    gd = POOL_GROUP_DIM
    xf = x_ref[0, rows, :]
    hn = (xf * _rms_scale(xf) * g_ref[...]).astype(BF16)

    u = jnp.dot(hn, w_ref[:, 3 * D_ATTN:], preferred_element_type=F32)
    ubuf[POOL_PAD:top, :] = ubuf[ts + POOL_PAD:ts + top, :]
    ubuf[top:top + ts, :] = u
    ext = ts + POOL_HALO
    s2 = ubuf[POOL_PAD:POOL_PAD + ext, :] + ubuf[pl.ds(POOL_PAD - 1, ext), :]
    lvl1[POOL_PAD:POOL_PAD + ext, :] = s2
    s4 = s2[:, gd:] + lvl1[pl.ds(POOL_PAD - 2, ext), gd:]
    lvl2[POOL_PAD:POOL_PAD + ext, :] = s4
    s8 = s4[:, gd:] + lvl2[pl.ds(POOL_PAD - 4, ext), gd:]
    lvl3[POOL_PAD:POOL_PAD + ext, :] = s8
    s16 = s8[POOL_HALO:, gd:] + lvl3[top - 8:top - 8 + ts, gd:]
    sums = (s2[POOL_HALO:, 0:gd], s4[POOL_HALO:, 0:gd], s8[POOL_HALO:, 0:gd], s16)
    t_pos = t0 + lax.broadcasted_iota(jnp.int32, (ts, gd), 0)
    for g, w in enumerate(POOL_WINDOWS):
        cols = slice(g * gd, (g + 1) * gd)
        cnt = jnp.minimum(t_pos + 1, w).astype(F32)
        diff = (sums[g] / cnt - u[:, cols]).astype(BF16)
        yg = jnp.dot(diff, wp_ref[g].astype(BF16), preferred_element_type=F32)
        yg = (yg + bp_ref[:, cols]) * ps_ref[:, cols]
        yp_ref[0, rows, cols] = yg.astype(BF16)

    reps = D_ATTN // cos_ref.shape[1]
    cos = jnp.concatenate([cos_ref[rows, :]] * reps, axis=1)
    sin = jnp.concatenate([sin_ref[rows, :]] * reps, axis=1)
    lane = lax.broadcasted_iota(jnp.int32, (ts, D_ATTN), 1)
    first_half = (lane % HEAD_DIM) < (HEAD_DIM // 2)

    def rope(t):
        partner = jnp.where(first_half,
                            pltpu.roll(t, D_ATTN - HEAD_DIM // 2, axis=1),
                            pltpu.roll(t, HEAD_DIM // 2, axis=1))
        return t * cos + partner * sin

    qf = jnp.dot(hn, w_ref[:, 0:D_ATTN], preferred_element_type=F32)
    qt_ref[0, :, rows] = (rope(qf) * (HEAD_DIM ** -0.5 * LOG2_E)).T.astype(BF16)
    kf = jnp.dot(hn, w_ref[:, D_ATTN:2 * D_ATTN], preferred_element_type=F32)
    k_ref[0, rows, :] = rope(kf).astype(BF16)
    vf = jnp.dot(hn, w_ref[:, 2 * D_ATTN:3 * D_ATTN], preferred_element_type=F32)
    vt_ref[0, :, rows] = vf.T.astype(BF16)


def _mix_in(x, g_pre, w_in, cos_t, sin_t, w_pool, b_pool, pool_scale, later_weights):
    B, S, _ = x.shape
    ts = TILES_PER_STEP * SEQ_TILE
    n_s = S // ts
    n_steps = B * n_s
    lvl_rows = POOL_PAD + POOL_HALO + SEQ_TILE
    gd = POOL_GROUP_DIM
    slabs = []
    for w in later_weights:
        assert w.shape[0] % (n_steps * V7X_BF16_ROWS) == 0, w.shape
        slabs.append((w.shape[0] // n_steps, w.shape[1]))
    blocks = (_nbytes((ts, D_MODEL), F32) + 2 * _nbytes((ts, V7X_LANES), F32)
              + _nbytes(w_pool.shape, F32) + 4 * _nbytes((ts, D_ATTN), BF16)
              + sum(_nbytes(sl, F32) + _nbytes(sl, BF16) for sl in slabs))
    scratch = (_nbytes((lvl_rows, 4 * D_POOL - 3 * gd), F32) + _nbytes(w_in.shape, F32)
               + _nbytes(w_in.shape, BF16))
    temps = 12 * _nbytes((SEQ_TILE, D_ATTN), F32)
    row = lambda b, s: (0, 0)
    step = lambda b, s: (b * n_s + s, 0)
    return pl.pallas_call(
        _mix_in_kernel,
        name="mix_in",
        grid=(B, n_s),
        in_specs=[
            pl.BlockSpec((1, ts, D_MODEL), lambda b, s: (b, s, 0)),
            pl.BlockSpec((1, D_MODEL), row),
            pl.BlockSpec(w_in.shape, row, pipeline_mode=pl.Buffered(1)),
            pl.BlockSpec((ts, V7X_LANES), lambda b, s: (s, 0)),
            pl.BlockSpec((ts, V7X_LANES), lambda b, s: (s, 0)),
            pl.BlockSpec(w_pool.shape, lambda b, s: (0, 0, 0)),
            pl.BlockSpec((1, D_POOL), row),
            pl.BlockSpec((1, D_POOL), row),
        ] + [pl.BlockSpec(sl, step) for sl in slabs],
        out_specs=[
            pl.BlockSpec((1, D_ATTN, ts), lambda b, s: (b, 0, s)),
            pl.BlockSpec((1, ts, D_ATTN), lambda b, s: (b, s, 0)),
            pl.BlockSpec((1, D_ATTN, ts), lambda b, s: (b, 0, s)),
            pl.BlockSpec((1, ts, D_POOL), lambda b, s: (b, s, 0)),
        ] + [pl.BlockSpec(sl, step) for sl in slabs],
        out_shape=[
            jax.ShapeDtypeStruct((B, D_ATTN, S), BF16),
            jax.ShapeDtypeStruct((B, S, D_ATTN), BF16),
            jax.ShapeDtypeStruct((B, D_ATTN, S), BF16),
            jax.ShapeDtypeStruct((B, S, D_POOL), BF16),
        ] + [jax.ShapeDtypeStruct(w.shape, BF16) for w in later_weights],
        scratch_shapes=[
            pltpu.VMEM(w_in.shape, BF16),
            pltpu.VMEM((lvl_rows, D_POOL), F32),
            pltpu.VMEM((lvl_rows, D_POOL), F32),
            pltpu.VMEM((lvl_rows, D_POOL - gd), F32),
            pltpu.VMEM((lvl_rows, D_POOL - 2 * gd), F32),
        ],
        compiler_params=pltpu.CompilerParams(
            dimension_semantics=("arbitrary", "arbitrary"),
            vmem_limit_bytes=_vmem_limit(blocks, scratch, temps)),
    )(x, g_pre, w_in, cos_t, sin_t, w_pool, b_pool, pool_scale, *later_weights)


def _moba_kernel(qt_ref, k_ref, vt_ref, o_ref, s_scr0, s_scr1, p_scr0, p_scr1,
                 diag_lo, diag_hi):
    slab = qt_ref.shape[1]
    S = qt_ref.shape[2]
    blk = MOBA_BLOCK
    nb = S // blk
    n_sel = min(MOBA_TOPK, nb - 1)
    sq = MOBA_QUERY_TILE
    per = sq // blk

    lane = lax.broadcasted_iota(jnp.int32, (1, slab), 1)
    kmean = jnp.concatenate(
        [jnp.sum(k_ref[0, j * blk:(j + 1) * blk, :].astype(F32), axis=0, keepdims=True)
         for j in range(nb)], axis=0) * (1.0 / blk)

    blk_id = lax.broadcasted_iota(jnp.int32, (nb, S), 0)
    q_blk = lax.broadcasted_iota(jnp.int32, (nb, S), 1) // blk
    past = blk_id < q_blk
    key_row = lax.broadcasted_iota(jnp.int32, (blk, sq), 0)
    qry_col = lax.broadcasted_iota(jnp.int32, (blk, sq), 1)
    diag_lo[...] = jnp.where(key_row <= qry_col, 0.0, NEG_INF)
    diag_hi[...] = jnp.where(key_row + blk <= qry_col, 0.0, NEG_INF)
    second_block = lax.broadcasted_iota(jnp.int32, (1, sq), 1) >= blk
    row_base = jnp.minimum(pl.program_id(0), 0)
    n_heads = slab // HEAD_DIM

    gate_lhs = []
    for h in range(n_heads):
        km = jnp.where((lane // HEAD_DIM) == h, kmean, 0.0)
        km_hi = km.astype(BF16)
        gate_lhs += [km_hi, (km - km_hi.astype(F32)).astype(BF16)]
    g2 = jnp.dot(jnp.concatenate(gate_lhs, axis=0), qt_ref[0], preferred_element_type=F32)

    def selection_bias(h):
        g_hi = g2[2 * h * nb:(2 * h + 1) * nb]
        g_lo = g2[(2 * h + 1) * nb:(2 * h + 2) * nb]
        gate = jnp.where(past, g_hi + g_lo, -jnp.inf)
        rank = jnp.zeros((nb, S), jnp.int32)
        for other in range(nb):
            row = gate[other:other + 1, :]
            beats = jnp.where(row > gate, 1,
                              jnp.where(jnp.logical_and(row == gate, other < blk_id), 1, 0))
            rank = rank + beats
        return jnp.where(jnp.logical_and(rank < n_sel, past), 0.0, NEG_INF)

    class Stage:
        def __init__(self, index, h, t, bias):
            self.h, self.t, self.bias = h, t, bias
            self.s_scr = (s_scr0, s_scr1)[index % 2]
            self.p_scr = (p_scr0, p_scr1)[index % MOBA_VALUE_LAG]
            self.qcols = slice(t * sq, (t + 1) * sq)
            self.n_kv = per * (t + 1)
            self.head = slice(h * HEAD_DIM, (h + 1) * HEAD_DIM)
            pieces = []
            if h > 0:
                pieces.append(jnp.zeros((h * HEAD_DIM, sq), BF16))
            pieces.append(qt_ref[0, self.head, self.qcols])
            if h < n_heads - 1:
                pieces.append(jnp.zeros(((n_heads - 1 - h) * HEAD_DIM, sq), BF16))
            self.qm = jnp.concatenate(pieces, axis=0)
            self.m_tile = None
            self.m = None

        def row_bias(self, j):
            if j == per * self.t:
                return jnp.where(second_block, self.bias[j:j + 1, self.qcols], 0.0)
            if j == per * self.t + 1:
                return None
            return self.bias[j:j + 1, self.qcols]

        def score_matmul(self, j):
            return jnp.dot(k_ref[0, j * blk:(j + 1) * blk, :], self.qm,
                           preferred_element_type=F32)

        def score_rows(self, j, sj, part):
            sub = slice(part * MOBA_SUB_ROWS, (part + 1) * MOBA_SUB_ROWS)
            s = sj[sub, :]
            if j == per * self.t:
                s = s + diag_lo[sub, :]
            elif j == per * self.t + 1:
                s = s + diag_hi[sub, :]
            r0 = j * blk + part * MOBA_SUB_ROWS
            self.s_scr[r0:r0 + MOBA_SUB_ROWS, :] = s
            mj = jnp.max(s.reshape(MOBA_SUB_ROWS // V7X_SUBLANES, V7X_SUBLANES, sq), axis=0)
            rb = self.row_bias(j)
            if rb is not None:
                mj = mj + rb
            self.m_tile = mj if self.m_tile is None else jnp.maximum(self.m_tile, mj)

        def prob_rows(self, j, part):
            if self.m is None:
                self.m = jnp.max(self.m_tile, axis=0, keepdims=True)
            rb = self.row_bias(j)
            shift = self.m if rb is None else self.m - rb
            r0 = j * blk + part * MOBA_SUB_ROWS
            s = self.s_scr[pl.ds(pl.multiple_of(row_base + r0, MOBA_SUB_ROWS), MOBA_SUB_ROWS), :]
            self.p_scr[r0:r0 + MOBA_SUB_ROWS, :] = jnp.exp2(s - shift).astype(BF16)

        def weighted_values(self):
            kv = self.n_kv * blk
            vt = jnp.concatenate([vt_ref[0, self.head, 0:kv],
                                  jnp.ones((V7X_BF16_ROWS, kv), BF16)], axis=0)
            p = self.p_scr[pl.ds(pl.multiple_of(row_base, blk), kv), :]
            o = jnp.dot(vt, p, preferred_element_type=F32)
            l = o[HEAD_DIM:HEAD_DIM + 1, :]
            o_ref[0, self.head, self.qcols] = (o[0:HEAD_DIM, :] / l).astype(BF16)

    stages = []
    for h in range(n_heads):
        bias = selection_bias(h)
        for t in range(S // sq):
            stages.append((h, t, bias))
    parts = blk // MOBA_SUB_ROWS
    live = {}
    for k in range(len(stages) + MOBA_VALUE_LAG):
        if k - MOBA_VALUE_LAG in live:
            live.pop(k - MOBA_VALUE_LAG).weighted_values()
        prev = live.get(k - 1)
        pending = ([(j, part) for j in range(prev.n_kv) for part in range(parts)]
                   if prev is not None else [])
        if k < len(stages):
            cur = live[k] = Stage(k, *stages[k])
            for j in range(cur.n_kv):
                sj = cur.score_matmul(j)
                for part in range(parts):
                    cur.score_rows(j, sj, part)
                    if pending:
                        prev.prob_rows(*pending.pop(0))
        for j, part in pending:
            prev.prob_rows(j, part)


def _moba(qt, k, vt):
    B, _, S = qt.shape
    slab = HEADS_PER_STEP * HEAD_DIM
    blocks = 4 * _nbytes((S, slab), BF16)
    sq = MOBA_QUERY_TILE
    score_bufs = [pltpu.VMEM((S, sq), F32)] * 2 + [pltpu.VMEM((S, sq), BF16)] * MOBA_VALUE_LAG
    diag_bufs = [pltpu.VMEM((MOBA_BLOCK, sq), F32)] * 2
    scratch = (2 * _nbytes((S, sq), F32) + MOBA_VALUE_LAG * _nbytes((S, sq), BF16)
               + 2 * _nbytes((MOBA_BLOCK, sq), F32))
    temps = _nbytes((S, sq), F32) + 8 * _nbytes((S // MOBA_BLOCK * 2, S), F32)
    return pl.pallas_call(
        _moba_kernel,
        name="moba",
        grid=(B, D_ATTN // slab),
        in_specs=[
            pl.BlockSpec((1, slab, S), lambda b, c: (b, c, 0)),
            pl.BlockSpec((1, S, slab), lambda b, c: (b, 0, c)),
            pl.BlockSpec((1, slab, S), lambda b, c: (b, c, 0)),
        ],
        out_specs=pl.BlockSpec((1, slab, S), lambda b, c: (b, c, 0)),
        out_shape=jax.ShapeDtypeStruct((B, D_ATTN, S), BF16),
        scratch_shapes=score_bufs + diag_bufs,
        compiler_params=pltpu.CompilerParams(
            dimension_semantics=("parallel", "parallel"),
            vmem_limit_bytes=_vmem_limit(blocks, scratch, temps)),
    )(qt, k, vt)


def _mix_out_kernel(yt_ref, yp_ref, x_ref, wo_ref, g_ref, o_ref):
    ya = yt_ref[0].T
    mix = jnp.dot(ya, wo_ref[0:D_ATTN, :], preferred_element_type=F32)
    mix = mix + jnp.dot(yp_ref[0], wo_ref[D_ATTN:, :], preferred_element_type=F32)
    o_ref[0] = x_ref[0] + mix * _rms_scale(mix) * g_ref[...]


def _mix_out(yt, yp, x, w_out, g_post):
    B, S, _ = x.shape
    ts = SEQ_TILE
    blocks = (2 * _nbytes((ts, D_ATTN), BF16) + 2 * _nbytes((ts, D_MODEL), F32)
              + _nbytes(w_out.shape, BF16))
    temps = 4 * _nbytes((ts, D_MODEL), F32)
    row = lambda b, s: (0, 0)
    return pl.pallas_call(
        _mix_out_kernel,
        name="mix_out",
        grid=(B, S // ts),
        in_specs=[
            pl.BlockSpec((1, D_ATTN, ts), lambda b, s: (b, 0, s)),
            pl.BlockSpec((1, ts, D_POOL), lambda b, s: (b, s, 0)),
            pl.BlockSpec((1, ts, D_MODEL), lambda b, s: (b, s, 0)),
            pl.BlockSpec(w_out.shape, row),
            pl.BlockSpec((1, D_MODEL), row),
        ],
        out_specs=pl.BlockSpec((1, ts, D_MODEL), lambda b, s: (b, s, 0)),
        out_shape=jax.ShapeDtypeStruct((B, S, D_MODEL), F32),
        compiler_params=pltpu.CompilerParams(
            dimension_semantics=("parallel", "parallel"),
            vmem_limit_bytes=_vmem_limit(blocks, 0, temps)),
    )(yt, yp, x, w_out, g_post)


def _conv_ffn_kernel(x_ref, gpre_ref, wup_ref, cw_ref, cb_ref, wd_ref, gpost_ref, o_ref,
                     h_scr, abuf0, abuf1, carry, z_scr):
    r = pl.program_id(1)
    rows = x_ref.shape[1]
    tf = FF_CHUNK
    n_chunks = D_FF // tf

    @pl.when(r == 0)
    def _():
        carry[...] = jnp.zeros_like(carry)

    x1 = x_ref[0]
    h_scr[...] = (x1 * _rms_scale(x1) * gpre_ref[...]).astype(BF16)
    half_gate = jnp.where(lax.broadcasted_iota(jnp.int32, (1, 2 * tf), 1) < tf, 0.5, 1.0)

    for c in range(n_chunks):
        abuf = (abuf0, abuf1)[c % 2]
        gate_cols = slice(c * tf, (c + 1) * tf)
        val_cols = slice(D_FF + c * tf, D_FF + (c + 1) * tf)

        def pair(ref, rows_=slice(None)):
            return jnp.concatenate([ref[rows_, gate_cols], ref[rows_, val_cols]], axis=1)

        a = jnp.dot(h_scr[...], pair(wup_ref), preferred_element_type=F32)
        abuf[0:CONV_HALO, :] = carry[c]
        abuf[CONV_HALO:CONV_HALO + rows, :] = a
        carry[c] = a[rows - CONV_HALO:rows, :]
        conv = a * (pair(cw_ref, slice(CONV_WIDTH - 1, CONV_WIDTH)) * half_gate) + pair(cb_ref) * half_gate
        for back in range(1, CONV_WIDTH):
            tap = CONV_WIDTH - 1 - back
            conv = conv + (abuf[pl.ds(CONV_HALO - back, rows), :]
                           * (pair(cw_ref, slice(tap, tap + 1)) * half_gate))
        gate_half = conv[:, 0:tf]
        val = conv[:, tf:2 * tf]
        z_scr[:, c * tf:(c + 1) * tf] = (
            (gate_half + gate_half * jnp.tanh(gate_half)) * val).astype(BF16)

    f = jnp.dot(z_scr[...], wd_ref[...], preferred_element_type=F32)
    o_ref[0] = x_ref[0] + f * _rms_scale(f) * gpost_ref[...]


def _conv_ffn(x1, g_pre, wup, cw, cb, wd, g_post):
    B, S, _ = x1.shape
    rows = SEQ_TILE
    tf = FF_CHUNK
    n_chunks = D_FF // tf
    blocks = 2 * _nbytes((rows, D_MODEL), F32) + _nbytes((V7X_SUBLANES + 1, 2 * D_FF), F32)
    resident = _nbytes(wup.shape, BF16) + _nbytes(wd.shape, BF16)
    scratch = (_nbytes((rows, D_MODEL), BF16) + 2 * _nbytes((CONV_HALO + rows, 2 * tf), F32)
               + _nbytes((n_chunks, CONV_HALO, 2 * tf), F32) + _nbytes((rows, D_FF), BF16))
    temps = 6 * _nbytes((rows, 2 * tf), F32) + _nbytes((rows, D_MODEL), F32)
    row = lambda b, r: (0, 0)
    once = pl.Buffered(1)
    return pl.pallas_call(
        _conv_ffn_kernel,
        name="conv_ffn",
        grid=(B, S // rows),
        in_specs=[
            pl.BlockSpec((1, rows, D_MODEL), lambda b, r: (b, r, 0)),
            pl.BlockSpec((1, D_MODEL), row),
            pl.BlockSpec(wup.shape, row, pipeline_mode=once),
            pl.BlockSpec(cw.shape, row),
            pl.BlockSpec(cb.shape, row),
            pl.BlockSpec(wd.shape, row, pipeline_mode=once),
            pl.BlockSpec((1, D_MODEL), row),
        ],
        out_specs=pl.BlockSpec((1, rows, D_MODEL), lambda b, r: (b, r, 0)),
        out_shape=jax.ShapeDtypeStruct((B, S, D_MODEL), F32),
        scratch_shapes=[
            pltpu.VMEM((rows, D_MODEL), BF16),
            pltpu.VMEM((CONV_HALO + rows, 2 * tf), F32),
            pltpu.VMEM((CONV_HALO + rows, 2 * tf), F32),
            pltpu.VMEM((n_chunks, CONV_HALO, 2 * tf), F32),
            pltpu.VMEM((rows, D_FF), BF16),
        ],
        compiler_params=pltpu.CompilerParams(
            dimension_semantics=("parallel", "arbitrary"),
            vmem_limit_bytes=_vmem_limit(blocks, scratch + resident, temps)),
    )(x1, g_pre, wup, cw, cb, wd, g_post)


def _out_ffn_kernel(yt0_ref, yp0_ref, x0_ref, yt1_ref, yp1_ref, x1_ref, wo_ref, gmix_ref,
                    gpre_ref, wup_ref, cw_ref, cb_ref, wd_ref, gpost_ref, o_ref,
                    res_scr, h_scr, abuf0, abuf1, carry, z_scr, *, tiles_per_seq):
    i = pl.program_id(0)
    rows = o_ref.shape[1]
    tf = FF_CHUNK
    n_chunks = D_FF // tf
    slot = i % 2

    def mix_out_tile(yt_ref, yp_ref, x_ref, dst):
        mix = jnp.dot(yt_ref[0].T, wo_ref[0:D_ATTN, :], preferred_element_type=F32)
        mix = mix + jnp.dot(yp_ref[0], wo_ref[D_ATTN:, :], preferred_element_type=F32)
        res = x_ref[0] + mix * _rms_scale(mix) * gmix_ref[...]
        res_scr[dst] = res
        h_scr[dst] = (res * _rms_scale(res) * gpre_ref[...]).astype(BF16)

    @pl.when(i == 0)
    def _():
        mix_out_tile(yt0_ref, yp0_ref, x0_ref, 0)

    @pl.when(i % tiles_per_seq == 0)
    def _():
        carry[...] = jnp.zeros_like(carry)

    half_gate = jnp.where(lax.broadcasted_iota(jnp.int32, (1, 2 * tf), 1) < tf, 0.5, 1.0)

    for c in range(n_chunks):
        abuf = (abuf0, abuf1)[c % 2]
        gate_cols = slice(c * tf, (c + 1) * tf)
        val_cols = slice(D_FF + c * tf, D_FF + (c + 1) * tf)

        def pair(ref, rows_=slice(None)):
            return jnp.concatenate([ref[rows_, gate_cols], ref[rows_, val_cols]], axis=1)

        a = jnp.dot(h_scr[slot], pair(wup_ref), preferred_element_type=F32)
        abuf[0:CONV_HALO, :] = carry[c]
        abuf[CONV_HALO:CONV_HALO + rows, :] = a
        carry[c] = a[rows - CONV_HALO:rows, :]
        conv = a * (pair(cw_ref, slice(CONV_WIDTH - 1, CONV_WIDTH)) * half_gate) + pair(cb_ref) * half_gate
        for back in range(1, CONV_WIDTH):
            tap = CONV_WIDTH - 1 - back
            conv = conv + (abuf[pl.ds(CONV_HALO - back, rows), :]
                           * (pair(cw_ref, slice(tap, tap + 1)) * half_gate))
        gate_half = conv[:, 0:tf]
        val = conv[:, tf:2 * tf]
        z_scr[:, c * tf:(c + 1) * tf] = (
            (gate_half + gate_half * jnp.tanh(gate_half)) * val).astype(BF16)

    mix_out_tile(yt1_ref, yp1_ref, x1_ref, 1 - slot)
    f = jnp.dot(z_scr[...], wd_ref[...], preferred_element_type=F32)
    o_ref[0] = res_scr[slot] + f * _rms_scale(f) * gpost_ref[...]


def _out_ffn(yt, yp, x, w_out, g_mix, g_pre, wup, cw, cb, wd, g_post):
    B, S, _ = x.shape
    rows = SEQ_TILE
    tf = FF_CHUNK
    n_chunks = D_FF // tf
    per_seq = S // rows
    n_tiles = B * per_seq
    tile_blocks = (2 * _nbytes((rows, D_ATTN), BF16) + _nbytes((rows, D_MODEL), F32))
    blocks = (2 * tile_blocks + _nbytes((rows, D_MODEL), F32)
              + _nbytes((V7X_SUBLANES + 1, 2 * D_FF), F32))
    resident = _nbytes(wup.shape, BF16) + _nbytes(wd.shape, BF16) + _nbytes(w_out.shape, BF16)
    scratch = (2 * _nbytes((rows, D_MODEL), F32) + 2 * _nbytes((rows, D_MODEL), BF16)
               + 2 * _nbytes((CONV_HALO + rows, 2 * tf), F32)
               + _nbytes((n_chunks, CONV_HALO, 2 * tf), F32) + _nbytes((rows, D_FF), BF16))
    temps = 6 * _nbytes((rows, 2 * tf), F32) + 2 * _nbytes((rows, D_MODEL), F32)
    row = lambda i: (0, 0)
    once = pl.Buffered(1)

    def tile_specs(tile):
        return [
            pl.BlockSpec((1, D_ATTN, rows), lambda i: (tile(i) // per_seq, 0, tile(i) % per_seq)),
            pl.BlockSpec((1, rows, D_POOL), lambda i: (tile(i) // per_seq, tile(i) % per_seq, 0)),
            pl.BlockSpec((1, rows, D_MODEL), lambda i: (tile(i) // per_seq, tile(i) % per_seq, 0)),
        ]

    this_tile = lambda i: i
    next_tile = lambda i: jnp.minimum(i + 1, n_tiles - 1)
    return pl.pallas_call(
        functools.partial(_out_ffn_kernel, tiles_per_seq=per_seq),
        name="out_ffn",
        grid=(n_tiles,),
        in_specs=tile_specs(this_tile) + tile_specs(next_tile) + [
            pl.BlockSpec(w_out.shape, row, pipeline_mode=once),
            pl.BlockSpec((1, D_MODEL), row),
            pl.BlockSpec((1, D_MODEL), row),
            pl.BlockSpec(wup.shape, row, pipeline_mode=once),
            pl.BlockSpec(cw.shape, row),
            pl.BlockSpec(cb.shape, row),
            pl.BlockSpec(wd.shape, row, pipeline_mode=once),
            pl.BlockSpec((1, D_MODEL), row),
        ],
        out_specs=pl.BlockSpec((1, rows, D_MODEL), lambda i: (i // per_seq, i % per_seq, 0)),
        out_shape=jax.ShapeDtypeStruct((B, S, D_MODEL), F32),
        scratch_shapes=[
            pltpu.VMEM((2, rows, D_MODEL), F32),
            pltpu.VMEM((2, rows, D_MODEL), BF16),
            pltpu.VMEM((CONV_HALO + rows, 2 * tf), F32),
            pltpu.VMEM((CONV_HALO + rows, 2 * tf), F32),
            pltpu.VMEM((n_chunks, CONV_HALO, 2 * tf), F32),
            pltpu.VMEM((rows, D_FF), BF16),
        ],
        compiler_params=pltpu.CompilerParams(
            dimension_semantics=("arbitrary",),
            vmem_limit_bytes=_vmem_limit(blocks, scratch + resident, temps)),
    )(yt, yp, x, yt, yp, x, w_out, g_mix, g_pre, wup, cw, cb, wd, g_post)


def _rope_tables(seq_len):
    inv_freq = ROPE_THETA ** (-jnp.arange(0, HEAD_DIM, 2, dtype=F32) / HEAD_DIM)
    ang = jnp.arange(seq_len, dtype=F32)[:, None] * inv_freq[None, :]
    cos, sin = jnp.cos(ang), jnp.sin(ang)
    reps = V7X_LANES // HEAD_DIM
    cos_t = jnp.tile(jnp.concatenate([cos, cos], axis=1), (1, reps))
    sin_t = jnp.tile(jnp.concatenate([-sin, sin], axis=1), (1, reps))
    return cos_t, sin_t


def kernel(x, norm_mix_pre, w_in, w_pool, b_pool, pool_scale, w_out, norm_mix_post,
           norm_ffn_pre, w_up, conv_w, conv_b, w_down, norm_ffn_post):
    B, S, D = x.shape
    assert D == D_MODEL and S % (TILES_PER_STEP * SEQ_TILE) == 0 and D_FF % FF_CHUNK == 0
    assert S % MOBA_QUERY_TILE == 0 and MOBA_QUERY_TILE == 2 * MOBA_BLOCK
    depth = w_in.shape[0]
    cos_t, sin_t = _rope_tables(S)
    for l in range(depth):
        qt, k, vt, yp, w_out_bf, w_up_bf, w_down_bf = _mix_in(
            x, norm_mix_pre[l][None, :], w_in[l], cos_t, sin_t,
            w_pool[l], b_pool[l][None, :], pool_scale[l][None, :],
            (w_out[l], w_up[l], w_down[l]))
        yt = _moba(qt, k, vt)
        x = _out_ffn(
            yt, yp, x, w_out_bf, norm_mix_post[l][None, :], norm_ffn_pre[l][None, :], w_up_bf,
            conv_w[l].reshape(CONV_WIDTH, 2 * D_FF), conv_b[l][None, :],
            w_down_bf, norm_ffn_post[l][None, :])
    return x
```

```python
import math

import jax
import jax.numpy as jnp
from jax import lax
from jax.experimental import pallas as pl
from jax.experimental.pallas import tpu as pltpu

D_MODEL = 1024
N_ATTN_HEADS = 8
HEAD_DIM = 64
D_ATTN = N_ATTN_HEADS * HEAD_DIM
D_POOL = D_MODEL - D_ATTN
POOL_WINDOWS = (2, 4, 8, 16)
POOL_GROUP_DIM = D_POOL // len(POOL_WINDOWS)
D_IN_PROJ = 3 * D_ATTN + D_POOL
MOBA_BLOCK = 256
MOBA_TOPK = 3
ROPE_THETA = 10000.0
D_FF = ((8 * D_MODEL // 3 + 127) // 128) * 128
CONV_WIDTH = 3
RMS_EPS = 1e-6
NEG_INF = -1e30
LOG2_E = 1.4426950408889634

V7X_LANES = 128
V7X_SUBLANES = 8
V7X_BF16_ROWS = 16
V7X_MXU_DIM = 256
V7X_VMEM_BYTES = 64 * 1024 * 1024

SEQ_TILE = 512
TILES_PER_STEP = 2
HEADS_PER_STEP = V7X_MXU_DIM // HEAD_DIM
V7X_NUM_MXU = 2
MOBA_QUERY_TILE = V7X_NUM_MXU * MOBA_BLOCK
MOBA_SUB_ROWS = 64
MOBA_VALUE_LAG = 2
FF_CHUNK = 256
FFN_TILE = 1024
POOL_HALO = max(POOL_WINDOWS)
POOL_PAD = V7X_SUBLANES
CONV_HALO = V7X_SUBLANES

F32 = jnp.float32
BF16 = jnp.bfloat16


def _vmem_limit(block_bytes, scratch_bytes, temp_bytes):
    need = 2 * block_bytes + scratch_bytes + temp_bytes
    assert need <= V7X_VMEM_BYTES, need
    return int(need)


def _nbytes(shape, dtype):
    return math.prod(shape) * jnp.dtype(dtype).itemsize


def _rms_scale(v):
    return lax.rsqrt(jnp.mean(v * v, axis=-1, keepdims=True) + RMS_EPS)


def _mix_in_kernel(x_ref, g_ref, w32_ref, cos_ref, sin_ref, wp_ref, bp_ref, ps_ref,
                   wout32_ref, wup32_ref, wd32_ref,
                   qt_ref, k_ref, vt_ref, yp_ref, wout_ref, wup_ref, wd_ref,
                   w_ref, ubuf, lvl1, lvl2, lvl3):
    s = pl.program_id(1)
    ts = SEQ_TILE
    top = POOL_PAD + POOL_HALO
    gd = POOL_GROUP_DIM

    @pl.when(jnp.logical_and(pl.program_id(0) == 0, s == 0))
    def _():
        w_ref[...] = w32_ref[...].astype(BF16)

    @pl.when(s == 0)
    def _():
        ubuf[ts + POOL_PAD:ts + top, :] = jnp.zeros((POOL_HALO, D_POOL), F32)

    wout_ref[...] = wout32_ref[...].astype(BF16)
    wup_ref[...] = wup32_ref[...].astype(BF16)
    wd_ref[...] = wd32_ref[...].astype(BF16)

    ubuf[0:POOL_PAD, :] = jnp.zeros((POOL_PAD, D_POOL), F32)
    lvl1[0:POOL_PAD, :] = jnp.zeros((POOL_PAD, D_POOL), F32)
    lvl2[0:POOL_PAD, :] = jnp.zeros((POOL_PAD, D_POOL - gd), F32)
    for r0 in range(0, x_ref.shape[1], ts):
        _mix_in_tile(slice(r0, r0 + ts), s * x_ref.shape[1] + r0, x_ref, g_ref, w_ref, cos_ref,
                     sin_ref, wp_ref, bp_ref, ps_ref, qt_ref, k_ref, vt_ref, yp_ref,
                     ubuf, lvl1, lvl2, lvl3)


def _mix_in_tile(rows, t0, x_ref, g_ref, w_ref, cos_ref, sin_ref, wp_ref, bp_ref, ps_ref,
                 qt_ref, k_ref, vt_ref, yp_ref, ubuf, lvl1, lvl2, lvl3):
    ts = SEQ_TILE
    top = POOL_PAD + POOL_HALO
    gd = POOL_GROUP_DIM
    xf = x_ref[0, rows, :]
    hn = (xf * _rms_scale(xf) * g_ref[...]).astype(BF16)

    u = jnp.dot(hn, w_ref[:, 3 * D_ATTN:], preferred_element_type=F32)
    ubuf[POOL_PAD:top, :] = ubuf[ts + POOL_PAD:ts + top, :]
    ubuf[top:top + ts, :] = u
    ext = ts + POOL_HALO
    s2 = ubuf[POOL_PAD:POOL_PAD + ext, :] + ubuf[pl.ds(POOL_PAD - 1, ext), :]
    lvl1[POOL_PAD:POOL_PAD + ext, :] = s2
    s4 = s2[:, gd:] + lvl1[pl.ds(POOL_PAD - 2, ext), gd:]
    lvl2[POOL_PAD:POOL_PAD + ext, :] = s4
    s8 = s4[:, gd:] + lvl2[pl.ds(POOL_PAD - 4, ext), gd:]
    lvl3[POOL_PAD:POOL_PAD + ext, :] = s8
    s16 = s8[POOL_HALO:, gd:] + lvl3[top - 8:top - 8 + ts, gd:]
    sums = (s2[POOL_HALO:, 0:gd], s4[POOL_HALO:, 0:gd], s8[POOL_HALO:, 0:gd], s16)
    t_pos = t0 + lax.broadcasted_iota(jnp.int32, (ts, gd), 0)
    for g, w in enumerate(POOL_WINDOWS):
        cols = slice(g * gd, (g + 1) * gd)
        cnt = jnp.minimum(t_pos + 1, w).astype(F32)
        diff = (sums[g] / cnt - u[:, cols]).astype(BF16)
        yg = jnp.dot(diff, wp_ref[g].astype(BF16), preferred_element_type=F32)
        yg = (yg + bp_ref[:, cols]) * ps_ref[:, cols]
        yp_ref[0, rows, cols] = yg.astype(BF16)

    reps = D_ATTN // cos_ref.shape[1]
    cos = jnp.concatenate([cos_ref[rows, :]] * reps, axis=1)
    sin = jnp.concatenate([sin_ref[rows, :]] * reps, axis=1)
    lane = lax.broadcasted_iota(jnp.int32, (ts, D_ATTN), 1)
    first_half = (lane % HEAD_DIM) < (HEAD_DIM // 2)

    def rope(t):
        partner = jnp.where(first_half,
                            pltpu.roll(t, D_ATTN - HEAD_DIM // 2, axis=1),
                            pltpu.roll(t, HEAD_DIM // 2, axis=1))
        return t * cos + partner * sin

    qf = jnp.dot(hn, w_ref[:, 0:D_ATTN], preferred_element_type=F32)
    qt_ref[0, :, rows] = (rope(qf) * (HEAD_DIM ** -0.5 * LOG2_E)).T.astype(BF16)
    kf = jnp.dot(hn, w_ref[:, D_ATTN:2 * D_ATTN], preferred_element_type=F32)
    k_ref[0, rows, :] = rope(kf).astype(BF16)
    vf = jnp.dot(hn, w_ref[:, 2 * D_ATTN:3 * D_ATTN], preferred_element_type=F32)
    vt_ref[0, :, rows] = vf.T.astype(BF16)


def _mix_in(x, g_pre, w_in, cos_t, sin_t, w_pool, b_pool, pool_scale, later_weights):
    B, S, _ = x.shape
    ts = TILES_PER_STEP * SEQ_TILE
    n_s = S // ts
    n_steps = B * n_s
    lvl_rows = POOL_PAD + POOL_HALO + SEQ_TILE
    gd = POOL_GROUP_DIM
    slabs = []
    for w in later_weights:
        assert w.shape[0] % (n_steps * V7X_BF16_ROWS) == 0, w.shape
        slabs.append((w.shape[0] // n_steps, w.shape[1]))
    blocks = (_nbytes((ts, D_MODEL), F32) + 2 * _nbytes((ts, V7X_LANES), F32)
              + _nbytes(w_pool.shape, F32) + 4 * _nbytes((ts, D_ATTN), BF16)
              + sum(_nbytes(sl, F32) + _nbytes(sl, BF16) for sl in slabs))
    scratch = (_nbytes((lvl_rows, 4 * D_POOL - 3 * gd), F32) + _nbytes(w_in.shape, F32)
               + _nbytes(w_in.shape, BF16))
    temps = 12 * _nbytes((SEQ_TILE, D_ATTN), F32)
    row = lambda b, s: (0, 0)
    step = lambda b, s: (b * n_s + s, 0)
    return pl.pallas_call(
        _mix_in_kernel,
        name="mix_in",
        grid=(B, n_s),
        in_specs=[
            pl.BlockSpec((1, ts, D_MODEL), lambda b, s: (b, s, 0)),
            pl.BlockSpec((1, D_MODEL), row),
            pl.BlockSpec(w_in.shape, row, pipeline_mode=pl.Buffered(1)),
            pl.BlockSpec((ts, V7X_LANES), lambda b, s: (s, 0)),
            pl.BlockSpec((ts, V7X_LANES), lambda b, s: (s, 0)),
            pl.BlockSpec(w_pool.shape, lambda b, s: (0, 0, 0)),
            pl.BlockSpec((1, D_POOL), row),
            pl.BlockSpec((1, D_POOL), row),
        ] + [pl.BlockSpec(sl, step) for sl in slabs],
        out_specs=[
            pl.BlockSpec((1, D_ATTN, ts), lambda b, s: (b, 0, s)),
            pl.BlockSpec((1, ts, D_ATTN), lambda b, s: (b, s, 0)),
            pl.BlockSpec((1, D_ATTN, ts), lambda b, s: (b, 0, s)),
            pl.BlockSpec((1, ts, D_POOL), lambda b, s: (b, s, 0)),
        ] + [pl.BlockSpec(sl, step) for sl in slabs],
        out_shape=[
            jax.ShapeDtypeStruct((B, D_ATTN, S), BF16),
            jax.ShapeDtypeStruct((B, S, D_ATTN), BF16),
            jax.ShapeDtypeStruct((B, D_ATTN, S), BF16),
            jax.ShapeDtypeStruct((B, S, D_POOL), BF16),
        ] + [jax.ShapeDtypeStruct(w.shape, BF16) for w in later_weights],
        scratch_shapes=[
            pltpu.VMEM(w_in.shape, BF16),
            pltpu.VMEM((lvl_rows, D_POOL), F32),
            pltpu.VMEM((lvl_rows, D_POOL), F32),
            pltpu.VMEM((lvl_rows, D_POOL - gd), F32),
            pltpu.VMEM((lvl_rows, D_POOL - 2 * gd), F32),
        ],
        compiler_params=pltpu.CompilerParams(
            dimension_semantics=("arbitrary", "arbitrary"),
            vmem_limit_bytes=_vmem_limit(blocks, scratch, temps)),
    )(x, g_pre, w_in, cos_t, sin_t, w_pool, b_pool, pool_scale, *later_weights)


def _moba_kernel(qt_ref, k_ref, vt_ref, o_ref, s_scr0, s_scr1, p_scr0, p_scr1,
                 diag_lo, diag_hi):
    slab = qt_ref.shape[1]
    S = qt_ref.shape[2]
    blk = MOBA_BLOCK
    nb = S // blk
    n_sel = min(MOBA_TOPK, nb - 1)
    sq = MOBA_QUERY_TILE
    per = sq // blk

    lane = lax.broadcasted_iota(jnp.int32, (1, slab), 1)
    kmean = jnp.concatenate(
        [jnp.sum(k_ref[0, j * blk:(j + 1) * blk, :].astype(F32), axis=0, keepdims=True)
         for j in range(nb)], axis=0) * (1.0 / blk)

    blk_id = lax.broadcasted_iota(jnp.int32, (nb, S), 0)
    q_blk = lax.broadcasted_iota(jnp.int32, (nb, S), 1) // blk
    past = blk_id < q_blk
    key_row = lax.broadcasted_iota(jnp.int32, (blk, sq), 0)
    qry_col = lax.broadcasted_iota(jnp.int32, (blk, sq), 1)
    diag_lo[...] = jnp.where(key_row <= qry_col, 0.0, NEG_INF)
    diag_hi[...] = jnp.where(key_row + blk <= qry_col, 0.0, NEG_INF)
    second_block = lax.broadcasted_iota(jnp.int32, (1, sq), 1) >= blk
    row_base = jnp.minimum(pl.program_id(0), 0)
    n_heads = slab // HEAD_DIM

    gate_lhs = []
    for h in range(n_heads):
        km = jnp.where((lane // HEAD_DIM) == h, kmean, 0.0)
        km_hi = km.astype(BF16)
        gate_lhs += [km_hi, (km - km_hi.astype(F32)).astype(BF16)]
    g2 = jnp.dot(jnp.concatenate(gate_lhs, axis=0), qt_ref[0], preferred_element_type=F32)

    def selection_bias(h):
        g_hi = g2[2 * h * nb:(2 * h + 1) * nb]
        g_lo = g2[(2 * h + 1) * nb:(2 * h + 2) * nb]
        gate = jnp.where(past, g_hi + g_lo, -jnp.inf)
        rank = jnp.zeros((nb, S), jnp.int32)
        for other in range(nb):
            row = gate[other:other + 1, :]
            beats = jnp.where(row > gate, 1,
                              jnp.where(jnp.logical_and(row == gate, other < blk_id), 1, 0))
            rank = rank + beats
        return jnp.where(jnp.logical_and(rank < n_sel, past), 0.0, NEG_INF)

    class Stage:
        def __init__(self, index, h, t, bias):
            self.h, self.t, self.bias = h, t, bias
            self.s_scr = (s_scr0, s_scr1)[index % 2]
            self.p_scr = (p_scr0, p_scr1)[index % MOBA_VALUE_LAG]
            self.qcols = slice(t * sq, (t + 1) * sq)
            self.n_kv = per * (t + 1)
            self.head = slice(h * HEAD_DIM, (h + 1) * HEAD_DIM)
            pieces = []
            if h > 0:
                pieces.append(jnp.zeros((h * HEAD_DIM, sq), BF16))
            pieces.append(qt_ref[0, self.head, self.qcols])
            if h < n_heads - 1:
                pieces.append(jnp.zeros(((n_heads - 1 - h) * HEAD_DIM, sq), BF16))
            self.qm = jnp.concatenate(pieces, axis=0)
            self.m_tile = None
            self.m = None

        def row_bias(self, j):
            if j == per * self.t:
                return jnp.where(second_block, self.bias[j:j + 1, self.qcols], 0.0)
            if j == per * self.t + 1:
                return None
            return self.bias[j:j + 1, self.qcols]

        def score_matmul(self, j):
            return jnp.dot(k_ref[0, j * blk:(j + 1) * blk, :], self.qm,
                           preferred_element_type=F32)

        def score_rows(self, j, sj, part):
            sub = slice(part * MOBA_SUB_ROWS, (part + 1) * MOBA_SUB_ROWS)
            s = sj[sub, :]
            if j == per * self.t:
                s = s + diag_lo[sub, :]
            elif j == per * self.t + 1:
                s = s + diag_hi[sub, :]
            r0 = j * blk + part * MOBA_SUB_ROWS
            self.s_scr[r0:r0 + MOBA_SUB_ROWS, :] = s
            mj = jnp.max(s.reshape(MOBA_SUB_ROWS // V7X_SUBLANES, V7X_SUBLANES, sq), axis=0)
            rb = self.row_bias(j)
            if rb is not None:
                mj = mj + rb
            self.m_tile = mj if self.m_tile is None else jnp.maximum(self.m_tile, mj)

        def prob_rows(self, j, part):
            if self.m is None:
                self.m = jnp.max(self.m_tile, axis=0, keepdims=True)
            rb = self.row_bias(j)
            shift = self.m if rb is None else self.m - rb
            r0 = j * blk + part * MOBA_SUB_ROWS
            s = self.s_scr[pl.ds(pl.multiple_of(row_base + r0, MOBA_SUB_ROWS), MOBA_SUB_ROWS), :]
            self.p_scr[r0:r0 + MOBA_SUB_ROWS, :] = jnp.exp2((s - shift).astype(BF16))

        def weighted_values(self):
            kv = self.n_kv * blk
            vt = jnp.concatenate([vt_ref[0, self.head, 0:kv],
                                  jnp.ones((V7X_BF16_ROWS, kv), BF16)], axis=0)
            p = self.p_scr[pl.ds(pl.multiple_of(row_base, blk), kv), :]
            o = jnp.dot(vt, p, preferred_element_type=F32)
            l = o[HEAD_DIM:HEAD_DIM + 1, :]
            o_ref[0, self.head, self.qcols] = (o[0:HEAD_DIM, :] / l).astype(BF16)

    stages = []
    for h in range(n_heads):
        bias = selection_bias(h)
        for t in range(S // sq):
            stages.append((h, t, bias))
    parts = blk // MOBA_SUB_ROWS
    live = {}
    for k in range(len(stages) + MOBA_VALUE_LAG):
        if k - MOBA_VALUE_LAG in live:
            live.pop(k - MOBA_VALUE_LAG).weighted_values()
        prev = live.get(k - 1)
        pending = ([(j, part) for j in range(prev.n_kv) for part in range(parts)]
                   if prev is not None else [])
        if k < len(stages):
            cur = live[k] = Stage(k, *stages[k])
            for j in range(cur.n_kv):
                sj = cur.score_matmul(j)
                for part in range(parts):
                    cur.score_rows(j, sj, part)
                    if pending:
                        prev.prob_rows(*pending.pop(0))
        for j, part in pending:
            prev.prob_rows(j, part)


def _moba(qt, k, vt):
    B, _, S = qt.shape
    slab = HEADS_PER_STEP * HEAD_DIM
    blocks = 4 * _nbytes((S, slab), BF16)
    sq = MOBA_QUERY_TILE
    score_bufs = [pltpu.VMEM((S, sq), F32)] * 2 + [pltpu.VMEM((S, sq), BF16)] * MOBA_VALUE_LAG
    diag_bufs = [pltpu.VMEM((MOBA_BLOCK, sq), F32)] * 2
    scratch = (2 * _nbytes((S, sq), F32) + MOBA_VALUE_LAG * _nbytes((S, sq), BF16)
               + 2 * _nbytes((MOBA_BLOCK, sq), F32))
    temps = _nbytes((S, sq), F32) + 8 * _nbytes((S // MOBA_BLOCK * 2, S), F32)
    return pl.pallas_call(
        _moba_kernel,
        name="moba",
        grid=(B, D_ATTN // slab),
        in_specs=[
            pl.BlockSpec((1, slab, S), lambda b, c: (b, c, 0)),
            pl.BlockSpec((1, S, slab), lambda b, c: (b, 0, c)),
            pl.BlockSpec((1, slab, S), lambda b, c: (b, c, 0)),
        ],
        out_specs=pl.BlockSpec((1, slab, S), lambda b, c: (b, c, 0)),
        out_shape=jax.ShapeDtypeStruct((B, D_ATTN, S), BF16),
        scratch_shapes=score_bufs + diag_bufs,
        compiler_params=pltpu.CompilerParams(
            dimension_semantics=("parallel", "parallel"),
            vmem_limit_bytes=_vmem_limit(blocks, scratch, temps)),
    )(qt, k, vt)


def _mix_out_kernel(yt_ref, yp_ref, x_ref, wo_ref, g_ref, o_ref):
    ya = yt_ref[0].T
    mix = jnp.dot(ya, wo_ref[0:D_ATTN, :], preferred_element_type=F32)
    mix = mix + jnp.dot(yp_ref[0], wo_ref[D_ATTN:, :], preferred_element_type=F32)
    o_ref[0] = x_ref[0] + mix * _rms_scale(mix) * g_ref[...]


def _mix_out(yt, yp, x, w_out, g_post):
    B, S, _ = x.shape
    ts = SEQ_TILE
    blocks = (2 * _nbytes((ts, D_ATTN), BF16) + 2 * _nbytes((ts, D_MODEL), F32)
              + _nbytes(w_out.shape, BF16))
    temps = 4 * _nbytes((ts, D_MODEL), F32)
    row = lambda b, s: (0, 0)
    return pl.pallas_call(
        _mix_out_kernel,
        name="mix_out",
        grid=(B, S // ts),
        in_specs=[
            pl.BlockSpec((1, D_ATTN, ts), lambda b, s: (b, 0, s)),
            pl.BlockSpec((1, ts, D_POOL), lambda b, s: (b, s, 0)),
            pl.BlockSpec((1, ts, D_MODEL), lambda b, s: (b, s, 0)),
            pl.BlockSpec(w_out.shape, row),
            pl.BlockSpec((1, D_MODEL), row),
        ],
        out_specs=pl.BlockSpec((1, ts, D_MODEL), lambda b, s: (b, s, 0)),
        out_shape=jax.ShapeDtypeStruct((B, S, D_MODEL), F32),
        compiler_params=pltpu.CompilerParams(
            dimension_semantics=("parallel", "parallel"),
            vmem_limit_bytes=_vmem_limit(blocks, 0, temps)),
    )(yt, yp, x, w_out, g_post)


def _conv_ffn_kernel(x_ref, gpre_ref, wup_ref, cw_ref, cb_ref, wd_ref, gpost_ref, o_ref,
                     h_scr, abuf0, abuf1, carry, z_scr):
    r = pl.program_id(1)
    rows = x_ref.shape[1]
    tf = FF_CHUNK
    n_chunks = D_FF // tf

    @pl.when(r == 0)
    def _():
        carry[...] = jnp.zeros_like(carry)

    x1 = x_ref[0]
    h_scr[...] = (x1 * _rms_scale(x1) * gpre_ref[...]).astype(BF16)
    half_gate = jnp.where(lax.broadcasted_iota(jnp.int32, (1, 2 * tf), 1) < tf, 0.5, 1.0)

    for c in range(n_chunks):
        abuf = (abuf0, abuf1)[c % 2]
        gate_cols = slice(c * tf, (c + 1) * tf)
        val_cols = slice(D_FF + c * tf, D_FF + (c + 1) * tf)

        def pair(ref, rows_=slice(None)):
            return jnp.concatenate([ref[rows_, gate_cols], ref[rows_, val_cols]], axis=1)

        a = jnp.dot(h_scr[...], pair(wup_ref), preferred_element_type=F32)
        abuf[0:CONV_HALO, :] = carry[c]
        abuf[CONV_HALO:CONV_HALO + rows, :] = a
        carry[c] = a[rows - CONV_HALO:rows, :]
        conv = a * (pair(cw_ref, slice(CONV_WIDTH - 1, CONV_WIDTH)) * half_gate) + pair(cb_ref) * half_gate
        for back in range(1, CONV_WIDTH):
            tap = CONV_WIDTH - 1 - back
            conv = conv + (abuf[pl.ds(CONV_HALO - back, rows), :]
                           * (pair(cw_ref, slice(tap, tap + 1)) * half_gate))
        gate_half = conv[:, 0:tf]
        val = conv[:, tf:2 * tf]
        z_scr[:, c * tf:(c + 1) * tf] = (
            (gate_half + gate_half * jnp.tanh(gate_half)) * val).astype(BF16)

    f = jnp.dot(z_scr[...], wd_ref[...], preferred_element_type=F32)
    o_ref[0] = x_ref[0] + f * _rms_scale(f) * gpost_ref[...]


def _conv_ffn(x1, g_pre, wup, cw, cb, wd, g_post):
    B, S, _ = x1.shape
    rows = FFN_TILE
    tf = FF_CHUNK
    n_chunks = D_FF // tf
    blocks = 2 * _nbytes((rows, D_MODEL), F32) + _nbytes((V7X_SUBLANES + 1, 2 * D_FF), F32)
    resident = _nbytes(wup.shape, BF16) + _nbytes(wd.shape, BF16)
    scratch = (_nbytes((rows, D_MODEL), BF16) + 2 * _nbytes((CONV_HALO + rows, 2 * tf), F32)
               + _nbytes((n_chunks, CONV_HALO, 2 * tf), F32) + _nbytes((rows, D_FF), BF16))
    temps = 6 * _nbytes((rows, 2 * tf), F32) + _nbytes((rows, D_MODEL), F32)
    row = lambda b, r: (0, 0)
    once = pl.Buffered(1)
    return pl.pallas_call(
        _conv_ffn_kernel,
        name="conv_ffn",
        grid=(B, S // rows),
        in_specs=[
            pl.BlockSpec((1, rows, D_MODEL), lambda b, r: (b, r, 0)),
            pl.BlockSpec((1, D_MODEL), row),
            pl.BlockSpec(wup.shape, row, pipeline_mode=once),
            pl.BlockSpec(cw.shape, row),
            pl.BlockSpec(cb.shape, row),
            pl.BlockSpec(wd.shape, row, pipeline_mode=once),
            pl.BlockSpec((1, D_MODEL), row),
        ],
        out_specs=pl.BlockSpec((1, rows, D_MODEL), lambda b, r: (b, r, 0)),
        out_shape=jax.ShapeDtypeStruct((B, S, D_MODEL), F32),
        scratch_shapes=[
            pltpu.VMEM((rows, D_MODEL), BF16),
            pltpu.VMEM((CONV_HALO + rows, 2 * tf), F32),
            pltpu.VMEM((CONV_HALO + rows, 2 * tf), F32),
            pltpu.VMEM((n_chunks, CONV_HALO, 2 * tf), F32),
            pltpu.VMEM((rows, D_FF), BF16),
        ],
        compiler_params=pltpu.CompilerParams(
            dimension_semantics=("parallel", "arbitrary"),
            vmem_limit_bytes=_vmem_limit(blocks, scratch + resident, temps)),
    )(x1, g_pre, wup, cw, cb, wd, g_post)


def _rope_tables(seq_len):
    inv_freq = ROPE_THETA ** (-jnp.arange(0, HEAD_DIM, 2, dtype=F32) / HEAD_DIM)
    ang = jnp.arange(seq_len, dtype=F32)[:, None] * inv_freq[None, :]
    cos, sin = jnp.cos(ang), jnp.sin(ang)
    reps = V7X_LANES // HEAD_DIM
    cos_t = jnp.tile(jnp.concatenate([cos, cos], axis=1), (1, reps))
    sin_t = jnp.tile(jnp.concatenate([-sin, sin], axis=1), (1, reps))
    return cos_t, sin_t


def kernel(x, norm_mix_pre, w_in, w_pool, b_pool, pool_scale, w_out, norm_mix_post,
           norm_ffn_pre, w_up, conv_w, conv_b, w_down, norm_ffn_post):
    B, S, D = x.shape
    assert D == D_MODEL and S % (TILES_PER_STEP * SEQ_TILE) == 0 and D_FF % FF_CHUNK == 0
    assert S % MOBA_QUERY_TILE == 0 and MOBA_QUERY_TILE == 2 * MOBA_BLOCK
    depth = w_in.shape[0]
    cos_t, sin_t = _rope_tables(S)
    for l in range(depth):
        qt, k, vt, yp, w_out_bf, w_up_bf, w_down_bf = _mix_in(
            x, norm_mix_pre[l][None, :], w_in[l], cos_t, sin_t,
            w_pool[l], b_pool[l][None, :], pool_scale[l][None, :],
            (w_out[l], w_up[l], w_down[l]))
        yt = _moba(qt, k, vt)
        x1 = _mix_out(yt, yp, x, w_out_bf, norm_mix_post[l][None, :])
        x = _conv_ffn(
            x1, norm_ffn_pre[l][None, :], w_up_bf,
            conv_w[l].reshape(CONV_WIDTH, 2 * D_FF), conv_b[l][None, :],
            w_down_bf, norm_ffn_post[l][None, :])
    return x
```

```python
import math

import jax
import jax.numpy as jnp
from jax import lax
from jax.experimental import pallas as pl
from jax.experimental.pallas import tpu as pltpu

D_MODEL = 1024
N_ATTN_HEADS = 8
HEAD_DIM = 64
D_ATTN = N_ATTN_HEADS * HEAD_DIM
D_POOL = D_MODEL - D_ATTN
POOL_WINDOWS = (2, 4, 8, 16)
POOL_GROUP_DIM = D_POOL // len(POOL_WINDOWS)
D_IN_PROJ = 3 * D_ATTN + D_POOL
MOBA_BLOCK = 256
MOBA_TOPK = 3
ROPE_THETA = 10000.0
D_FF = ((8 * D_MODEL // 3 + 127) // 128) * 128
CONV_WIDTH = 3
RMS_EPS = 1e-6
NEG_INF = -1e30
LOG2_E = 1.4426950408889634

V7X_LANES = 128
V7X_SUBLANES = 8
V7X_BF16_ROWS = 16
V7X_MXU_DIM = 256
V7X_VMEM_BYTES = 64 * 1024 * 1024

SEQ_TILE = 512
TILES_PER_STEP = 2
MIX_OUT_TILE = 1024
HEADS_PER_STEP = V7X_MXU_DIM // HEAD_DIM
V7X_NUM_MXU = 2
MOBA_QUERY_TILE = V7X_NUM_MXU * MOBA_BLOCK
MOBA_SUB_ROWS = 64
MOBA_VALUE_LAG = 2
FF_CHUNK = 256
POOL_HALO = max(POOL_WINDOWS)
POOL_PAD = V7X_SUBLANES
CONV_HALO = V7X_SUBLANES

F32 = jnp.float32
BF16 = jnp.bfloat16


def _vmem_limit(block_bytes, scratch_bytes, temp_bytes):
    need = 2 * block_bytes + scratch_bytes + temp_bytes
    assert need <= V7X_VMEM_BYTES, need
    return int(need)


def _nbytes(shape, dtype):
    return math.prod(shape) * jnp.dtype(dtype).itemsize


def _rms_scale(v):
    return lax.rsqrt(jnp.mean(v * v, axis=-1, keepdims=True) + RMS_EPS)


def _mix_in_kernel(x_ref, g_ref, w32_ref, cos_ref, sin_ref, wp_ref, bp_ref, ps_ref,
                   wout32_ref, wup32_ref, wd32_ref,
                   qt_ref, k_ref, vt_ref, yp_ref, wout_ref, wup_ref, wd_ref,
                   w_ref, ubuf, lvl1, lvl2, lvl3):
    s = pl.program_id(1)
    ts = SEQ_TILE
    top = POOL_PAD + POOL_HALO
    gd = POOL_GROUP_DIM

    @pl.when(jnp.logical_and(pl.program_id(0) == 0, s == 0))
    def _():
        w_ref[...] = w32_ref[...].astype(BF16)

    @pl.when(s == 0)
    def _():
        ubuf[ts + POOL_PAD:ts + top, :] = jnp.zeros((POOL_HALO, D_POOL), F32)

    wout_ref[...] = wout32_ref[...].astype(BF16)
    wup_ref[...] = wup32_ref[...].astype(BF16)
    wd_ref[...] = wd32_ref[...].astype(BF16)

    ubuf[0:POOL_PAD, :] = jnp.zeros((POOL_PAD, D_POOL), F32)
    lvl1[0:POOL_PAD, :] = jnp.zeros((POOL_PAD, D_POOL), F32)
    lvl2[0:POOL_PAD, :] = jnp.zeros((POOL_PAD, D_POOL - gd), F32)
    for r0 in range(0, x_ref.shape[1], ts):
        _mix_in_tile(slice(r0, r0 + ts), s * x_ref.shape[1] + r0, x_ref, g_ref, w_ref, cos_ref,
                     sin_ref, wp_ref, bp_ref, ps_ref, qt_ref, k_ref, vt_ref, yp_ref,
                     ubuf, lvl1, lvl2, lvl3)


def _mix_in_tile(rows, t0, x_ref, g_ref, w_ref, cos_ref, sin_ref, wp_ref, bp_ref, ps_ref,
                 qt_ref, k_ref, vt_ref, yp_ref, ubuf, lvl1, lvl2, lvl3):
    ts = SEQ_TILE
    top = POOL_PAD + POOL_HALO
    gd = POOL_GROUP_DIM
    xf = x_ref[0, rows, :]
    hn = (xf * _rms_scale(xf) * g_ref[...]).astype(BF16)

    u = jnp.dot(hn, w_ref[:, 3 * D_ATTN:], preferred_element_type=F32)
    ubuf[POOL_PAD:top, :] = ubuf[ts + POOL_PAD:ts + top, :]
    ubuf[top:top + ts, :] = u
    ext = ts + POOL_HALO
    s2 = ubuf[POOL_PAD:POOL_PAD + ext, :] + ubuf[pl.ds(POOL_PAD - 1, ext), :]
    lvl1[POOL_PAD:POOL_PAD + ext, :] = s2
    s4 = s2[:, gd:] + lvl1[pl.ds(POOL_PAD - 2, ext), gd:]
    lvl2[POOL_PAD:POOL_PAD + ext, :] = s4
    s8 = s4[:, gd:] + lvl2[pl.ds(POOL_PAD - 4, ext), gd:]
    lvl3[POOL_PAD:POOL_PAD + ext, :] = s8
    s16 = s8[POOL_HALO:, gd:] + lvl3[top - 8:top - 8 + ts, gd:]
    sums = (s2[POOL_HALO:, 0:gd], s4[POOL_HALO:, 0:gd], s8[POOL_HALO:, 0:gd], s16)
    t_pos = t0 + lax.broadcasted_iota(jnp.int32, (ts, gd), 0)
    for g, w in enumerate(POOL_WINDOWS):
        cols = slice(g * gd, (g + 1) * gd)
        cnt = jnp.minimum(t_pos + 1, w).astype(F32)
        diff = (sums[g] / cnt - u[:, cols]).astype(BF16)
        yg = jnp.dot(diff, wp_ref[g].astype(BF16), preferred_element_type=F32)
        yg = (yg + bp_ref[:, cols]) * ps_ref[:, cols]
        yp_ref[0, rows, cols] = yg.astype(BF16)

    reps = D_ATTN // cos_ref.shape[1]
    cos = jnp.concatenate([cos_ref[rows, :]] * reps, axis=1)
    sin = jnp.concatenate([sin_ref[rows, :]] * reps, axis=1)
    lane = lax.broadcasted_iota(jnp.int32, (ts, D_ATTN), 1)
    first_half = (lane % HEAD_DIM) < (HEAD_DIM // 2)

    def rope(t):
        partner = jnp.where(first_half,
                            pltpu.roll(t, D_ATTN - HEAD_DIM // 2, axis=1),
                            pltpu.roll(t, HEAD_DIM // 2, axis=1))
        return t * cos + partner * sin

    qf = jnp.dot(hn, w_ref[:, 0:D_ATTN], preferred_element_type=F32)
    qt_ref[0, :, rows] = (rope(qf) * (HEAD_DIM ** -0.5 * LOG2_E)).T.astype(BF16)
    kf = jnp.dot(hn, w_ref[:, D_ATTN:2 * D_ATTN], preferred_element_type=F32)
    k_ref[0, rows, :] = rope(kf).astype(BF16)
    vf = jnp.dot(hn, w_ref[:, 2 * D_ATTN:3 * D_ATTN], preferred_element_type=F32)
    vt_ref[0, :, rows] = vf.T.astype(BF16)


def _mix_in(x, g_pre, w_in, cos_t, sin_t, w_pool, b_pool, pool_scale, later_weights):
    B, S, _ = x.shape
    ts = TILES_PER_STEP * SEQ_TILE
    n_s = S // ts
    n_steps = B * n_s
    lvl_rows = POOL_PAD + POOL_HALO + SEQ_TILE
    gd = POOL_GROUP_DIM
    slabs = []
    for w in later_weights:
        assert w.shape[0] % (n_steps * V7X_BF16_ROWS) == 0, w.shape
        slabs.append((w.shape[0] // n_steps, w.shape[1]))
    blocks = (_nbytes((ts, D_MODEL), F32) + 2 * _nbytes((ts, V7X_LANES), F32)
              + _nbytes(w_pool.shape, F32) + 4 * _nbytes((ts, D_ATTN), BF16)
              + sum(_nbytes(sl, F32) + _nbytes(sl, BF16) for sl in slabs))
    scratch = (_nbytes((lvl_rows, 4 * D_POOL - 3 * gd), F32) + _nbytes(w_in.shape, F32)
               + _nbytes(w_in.shape, BF16))
    temps = 12 * _nbytes((SEQ_TILE, D_ATTN), F32)
    row = lambda b, s: (0, 0)
    step = lambda b, s: (b * n_s + s, 0)
    return pl.pallas_call(
        _mix_in_kernel,
        name="mix_in",
        grid=(B, n_s),
        in_specs=[
            pl.BlockSpec((1, ts, D_MODEL), lambda b, s: (b, s, 0)),
            pl.BlockSpec((1, D_MODEL), row),
            pl.BlockSpec(w_in.shape, row, pipeline_mode=pl.Buffered(1)),
            pl.BlockSpec((ts, V7X_LANES), lambda b, s: (s, 0)),
            pl.BlockSpec((ts, V7X_LANES), lambda b, s: (s, 0)),
            pl.BlockSpec(w_pool.shape, lambda b, s: (0, 0, 0)),
            pl.BlockSpec((1, D_POOL), row),
            pl.BlockSpec((1, D_POOL), row),
        ] + [pl.BlockSpec(sl, step) for sl in slabs],
        out_specs=[
            pl.BlockSpec((1, D_ATTN, ts), lambda b, s: (b, 0, s)),
            pl.BlockSpec((1, ts, D_ATTN), lambda b, s: (b, s, 0)),
            pl.BlockSpec((1, D_ATTN, ts), lambda b, s: (b, 0, s)),
            pl.BlockSpec((1, ts, D_POOL), lambda b, s: (b, s, 0)),
        ] + [pl.BlockSpec(sl, step) for sl in slabs],
        out_shape=[
            jax.ShapeDtypeStruct((B, D_ATTN, S), BF16),
            jax.ShapeDtypeStruct((B, S, D_ATTN), BF16),
            jax.ShapeDtypeStruct((B, D_ATTN, S), BF16),
            jax.ShapeDtypeStruct((B, S, D_POOL), BF16),
        ] + [jax.ShapeDtypeStruct(w.shape, BF16) for w in later_weights],
        scratch_shapes=[
            pltpu.VMEM(w_in.shape, BF16),
            pltpu.VMEM((lvl_rows, D_POOL), F32),
            pltpu.VMEM((lvl_rows, D_POOL), F32),
            pltpu.VMEM((lvl_rows, D_POOL - gd), F32),
            pltpu.VMEM((lvl_rows, D_POOL - 2 * gd), F32),
        ],
        compiler_params=pltpu.CompilerParams(
            dimension_semantics=("arbitrary", "arbitrary"),
            vmem_limit_bytes=_vmem_limit(blocks, scratch, temps)),
    )(x, g_pre, w_in, cos_t, sin_t, w_pool, b_pool, pool_scale, *later_weights)


def _moba_kernel(qt_ref, k_ref, vt_ref, o_ref, s_scr0, s_scr1, p_scr0, p_scr1,
                 diag_lo, diag_hi):
    slab = qt_ref.shape[1]
    S = qt_ref.shape[2]
    blk = MOBA_BLOCK
    nb = S // blk
    n_sel = min(MOBA_TOPK, nb - 1)
    sq = MOBA_QUERY_TILE
    per = sq // blk

    lane = lax.broadcasted_iota(jnp.int32, (1, slab), 1)
    kmean = jnp.concatenate(
        [jnp.sum(k_ref[0, j * blk:(j + 1) * blk, :].astype(F32), axis=0, keepdims=True)
         for j in range(nb)], axis=0) * (1.0 / blk)

    blk_id = lax.broadcasted_iota(jnp.int32, (nb, S), 0)
    q_blk = lax.broadcasted_iota(jnp.int32, (nb, S), 1) // blk
    past = blk_id < q_blk
    key_row = lax.broadcasted_iota(jnp.int32, (blk, sq), 0)
    qry_col = lax.broadcasted_iota(jnp.int32, (blk, sq), 1)
    diag_lo[...] = jnp.where(key_row <= qry_col, 0.0, NEG_INF)
    diag_hi[...] = jnp.where(key_row + blk <= qry_col, 0.0, NEG_INF)
    second_block = lax.broadcasted_iota(jnp.int32, (1, sq), 1) >= blk
    row_base = jnp.minimum(pl.program_id(0), 0)
    n_heads = slab // HEAD_DIM

    gate_lhs = []
    for h in range(n_heads):
        km = jnp.where((lane // HEAD_DIM) == h, kmean, 0.0)
        km_hi = km.astype(BF16)
        gate_lhs += [km_hi, (km - km_hi.astype(F32)).astype(BF16)]
    g2 = jnp.dot(jnp.concatenate(gate_lhs, axis=0), qt_ref[0], preferred_element_type=F32)

    def selection_bias(h):
        g_hi = g2[2 * h * nb:(2 * h + 1) * nb]
        g_lo = g2[(2 * h + 1) * nb:(2 * h + 2) * nb]
        gate = jnp.where(past, g_hi + g_lo, -jnp.inf)
        rank = jnp.zeros((nb, S), jnp.int32)
        for other in range(nb):
            row = gate[other:other + 1, :]
            beats = jnp.where(row > gate, 1,
                              jnp.where(jnp.logical_and(row == gate, other < blk_id), 1, 0))
            rank = rank + beats
        return jnp.where(jnp.logical_and(rank < n_sel, past), 0.0, NEG_INF)

    class Stage:
        def __init__(self, index, h, t, bias):
            self.h, self.t, self.bias = h, t, bias
            self.s_scr = (s_scr0, s_scr1)[index % 2]
            self.p_scr = (p_scr0, p_scr1)[index % MOBA_VALUE_LAG]
            self.qcols = slice(t * sq, (t + 1) * sq)
            self.n_kv = per * (t + 1)
            self.head = slice(h * HEAD_DIM, (h + 1) * HEAD_DIM)
            pieces = []
            if h > 0:
                pieces.append(jnp.zeros((h * HEAD_DIM, sq), BF16))
            pieces.append(qt_ref[0, self.head, self.qcols])
            if h < n_heads - 1:
                pieces.append(jnp.zeros(((n_heads - 1 - h) * HEAD_DIM, sq), BF16))
            self.qm = jnp.concatenate(pieces, axis=0)
            self.m_tile = None
            self.m = None

        def row_bias(self, j):
            if j == per * self.t:
                return jnp.where(second_block, self.bias[j:j + 1, self.qcols], 0.0)
            if j == per * self.t + 1:
                return None
            return self.bias[j:j + 1, self.qcols]

        def score_matmul(self, j):
            return jnp.dot(k_ref[0, j * blk:(j + 1) * blk, :], self.qm,
                           preferred_element_type=F32)

        def score_rows(self, j, sj, part):
            sub = slice(part * MOBA_SUB_ROWS, (part + 1) * MOBA_SUB_ROWS)
            s = sj[sub, :]
            if j == per * self.t:
                s = s + diag_lo[sub, :]
            elif j == per * self.t + 1:
                s = s + diag_hi[sub, :]
            r0 = j * blk + part * MOBA_SUB_ROWS
            self.s_scr[r0:r0 + MOBA_SUB_ROWS, :] = s
            mj = jnp.max(s.reshape(MOBA_SUB_ROWS // V7X_SUBLANES, V7X_SUBLANES, sq), axis=0)
            rb = self.row_bias(j)
            if rb is not None:
                mj = mj + rb
            self.m_tile = mj if self.m_tile is None else jnp.maximum(self.m_tile, mj)

        def prob_rows(self, j, part):
            if self.m is None:
                self.m = jnp.max(self.m_tile, axis=0, keepdims=True)
            rb = self.row_bias(j)
            shift = self.m if rb is None else self.m - rb
            r0 = j * blk + part * MOBA_SUB_ROWS
            s = self.s_scr[pl.ds(pl.multiple_of(row_base + r0, MOBA_SUB_ROWS), MOBA_SUB_ROWS), :]
            self.p_scr[r0:r0 + MOBA_SUB_ROWS, :] = jnp.exp2(s - shift).astype(BF16)

        def weighted_values(self):
            kv = self.n_kv * blk
            vt = jnp.concatenate([vt_ref[0, self.head, 0:kv],
                                  jnp.ones((V7X_BF16_ROWS, kv), BF16)], axis=0)
            p = self.p_scr[pl.ds(pl.multiple_of(row_base, blk), kv), :]
            o = jnp.dot(vt, p, preferred_element_type=F32)
            l = o[HEAD_DIM:HEAD_DIM + 1, :]
            o_ref[0, self.head, self.qcols] = (o[0:HEAD_DIM, :] / l).astype(BF16)

    stages = []
    for h in range(n_heads):
        bias = selection_bias(h)
        for t in range(S // sq):
            stages.append((h, t, bias))
    parts = blk // MOBA_SUB_ROWS
    live = {}
    for k in range(len(stages) + MOBA_VALUE_LAG):
        if k - MOBA_VALUE_LAG in live:
            live.pop(k - MOBA_VALUE_LAG).weighted_values()
        prev = live.get(k - 1)
        pending = ([(j, part) for j in range(prev.n_kv) for part in range(parts)]
                   if prev is not None else [])
        if k < len(stages):
            cur = live[k] = Stage(k, *stages[k])
            for j in range(cur.n_kv):
                sj = cur.score_matmul(j)
                for part in range(parts):
                    cur.score_rows(j, sj, part)
                    if pending:
                        prev.prob_rows(*pending.pop(0))
        for j, part in pending:
            prev.prob_rows(j, part)


def _moba(qt, k, vt):
    B, _, S = qt.shape
    slab = HEADS_PER_STEP * HEAD_DIM
    blocks = 4 * _nbytes((S, slab), BF16)
    sq = MOBA_QUERY_TILE
    score_bufs = [pltpu.VMEM((S, sq), F32)] * 2 + [pltpu.VMEM((S, sq), BF16)] * MOBA_VALUE_LAG
    diag_bufs = [pltpu.VMEM((MOBA_BLOCK, sq), F32)] * 2
    scratch = (2 * _nbytes((S, sq), F32) + MOBA_VALUE_LAG * _nbytes((S, sq), BF16)
               + 2 * _nbytes((MOBA_BLOCK, sq), F32))
    temps = _nbytes((S, sq), F32) + 8 * _nbytes((S // MOBA_BLOCK * 2, S), F32)
    return pl.pallas_call(
        _moba_kernel,
        name="moba",
        grid=(B, D_ATTN // slab),
        in_specs=[
            pl.BlockSpec((1, slab, S), lambda b, c: (b, c, 0)),
            pl.BlockSpec((1, S, slab), lambda b, c: (b, 0, c)),
            pl.BlockSpec((1, slab, S), lambda b, c: (b, c, 0)),
        ],
        out_specs=pl.BlockSpec((1, slab, S), lambda b, c: (b, c, 0)),
        out_shape=jax.ShapeDtypeStruct((B, D_ATTN, S), BF16),
        scratch_shapes=score_bufs + diag_bufs,
        compiler_params=pltpu.CompilerParams(
            dimension_semantics=("parallel", "parallel"),
            vmem_limit_bytes=_vmem_limit(blocks, scratch, temps)),
    )(qt, k, vt)


def _mix_out_kernel(yt_ref, yp_ref, x_ref, wo_ref, g_ref, o_ref):
    ya = yt_ref[0].T
    mix = jnp.dot(ya, wo_ref[0:D_ATTN, :], preferred_element_type=F32)
    mix = mix + jnp.dot(yp_ref[0], wo_ref[D_ATTN:, :], preferred_element_type=F32)
    o_ref[0] = x_ref[0] + mix * _rms_scale(mix) * g_ref[...]


def _mix_out(yt, yp, x, w_out, g_post):
    B, S, _ = x.shape
    ts = MIX_OUT_TILE
    blocks = (2 * _nbytes((ts, D_ATTN), BF16) + 2 * _nbytes((ts, D_MODEL), F32)
              + _nbytes(w_out.shape, BF16))
    temps = 4 * _nbytes((ts, D_MODEL), F32)
    row = lambda b, s: (0, 0)
    return pl.pallas_call(
        _mix_out_kernel,
        name="mix_out",
        grid=(B, S // ts),
        in_specs=[
            pl.BlockSpec((1, D_ATTN, ts), lambda b, s: (b, 0, s)),
            pl.BlockSpec((1, ts, D_POOL), lambda b, s: (b, s, 0)),
            pl.BlockSpec((1, ts, D_MODEL), lambda b, s: (b, s, 0)),
            pl.BlockSpec(w_out.shape, row),
            pl.BlockSpec((1, D_MODEL), row),
        ],
        out_specs=pl.BlockSpec((1, ts, D_MODEL), lambda b, s: (b, s, 0)),
        out_shape=jax.ShapeDtypeStruct((B, S, D_MODEL), F32),
        compiler_params=pltpu.CompilerParams(
            dimension_semantics=("parallel", "parallel"),
            vmem_limit_bytes=_vmem_limit(blocks, 0, temps)),
    )(yt, yp, x, w_out, g_post)


def _conv_ffn_kernel(x_ref, gpre_ref, wup_ref, cw_ref, cb_ref, wd_ref, gpost_ref, o_ref,
                     h_scr, abuf0, abuf1, carry, z_scr):
    r = pl.program_id(1)
    rows = x_ref.shape[1]
    tf = FF_CHUNK
    n_chunks = D_FF // tf

    @pl.when(r == 0)
    def _():
        carry[...] = jnp.zeros_like(carry)

    x1 = x_ref[0]
    h_scr[...] = (x1 * _rms_scale(x1) * gpre_ref[...]).astype(BF16)
    half_gate = jnp.where(lax.broadcasted_iota(jnp.int32, (1, 2 * tf), 1) < tf, 0.5, 1.0)

    for c in range(n_chunks):
        abuf = (abuf0, abuf1)[c % 2]
        gate_cols = slice(c * tf, (c + 1) * tf)
        val_cols = slice(D_FF + c * tf, D_FF + (c + 1) * tf)

        def pair(ref, rows_=slice(None)):
            return jnp.concatenate([ref[rows_, gate_cols], ref[rows_, val_cols]], axis=1)

        a = jnp.dot(h_scr[...], pair(wup_ref), preferred_element_type=F32)
        abuf[0:CONV_HALO, :] = carry[c]
        abuf[CONV_HALO:CONV_HALO + rows, :] = a
        carry[c] = a[rows - CONV_HALO:rows, :]
        conv = a * (pair(cw_ref, slice(CONV_WIDTH - 1, CONV_WIDTH)) * half_gate) + pair(cb_ref) * half_gate
        for back in range(1, CONV_WIDTH):
            tap = CONV_WIDTH - 1 - back
            conv = conv + (abuf[pl.ds(CONV_HALO - back, rows), :]
                           * (pair(cw_ref, slice(tap, tap + 1)) * half_gate))
        gate_half = conv[:, 0:tf]
        val = conv[:, tf:2 * tf]
        z_scr[:, c * tf:(c + 1) * tf] = (
            (gate_half + gate_half * jnp.tanh(gate_half)) * val).astype(BF16)

    f = jnp.dot(z_scr[...], wd_ref[...], preferred_element_type=F32)
    o_ref[0] = x_ref[0] + f * _rms_scale(f) * gpost_ref[...]


def _conv_ffn(x1, g_pre, wup, cw, cb, wd, g_post):
    B, S, _ = x1.shape
    rows = SEQ_TILE
    tf = FF_CHUNK
    n_chunks = D_FF // tf
    blocks = 2 * _nbytes((rows, D_MODEL), F32) + _nbytes((V7X_SUBLANES + 1, 2 * D_FF), F32)
    resident = _nbytes(wup.shape, BF16) + _nbytes(wd.shape, BF16)
    scratch = (_nbytes((rows, D_MODEL), BF16) + 2 * _nbytes((CONV_HALO + rows, 2 * tf), F32)
               + _nbytes((n_chunks, CONV_HALO, 2 * tf), F32) + _nbytes((rows, D_FF), BF16))
    temps = 6 * _nbytes((rows, 2 * tf), F32) + _nbytes((rows, D_MODEL), F32)
    row = lambda b, r: (0, 0)
    once = pl.Buffered(1)
    return pl.pallas_call(
        _conv_ffn_kernel,
        name="conv_ffn",
        grid=(B, S // rows),
        in_specs=[
            pl.BlockSpec((1, rows, D_MODEL), lambda b, r: (b, r, 0)),
            pl.BlockSpec((1, D_MODEL), row),
            pl.BlockSpec(wup.shape, row, pipeline_mode=once),
            pl.BlockSpec(cw.shape, row),
            pl.BlockSpec(cb.shape, row),
            pl.BlockSpec(wd.shape, row, pipeline_mode=once),
            pl.BlockSpec((1, D_MODEL), row),
        ],
        out_specs=pl.BlockSpec((1, rows, D_MODEL), lambda b, r: (b, r, 0)),
        out_shape=jax.ShapeDtypeStruct((B, S, D_MODEL), F32),
        scratch_shapes=[
            pltpu.VMEM((rows, D_MODEL), BF16),
            pltpu.VMEM((CONV_HALO + rows, 2 * tf), F32),
            pltpu.VMEM((CONV_HALO + rows, 2 * tf), F32),
            pltpu.VMEM((n_chunks, CONV_HALO, 2 * tf), F32),
            pltpu.VMEM((rows, D_FF), BF16),
        ],
        compiler_params=pltpu.CompilerParams(
            dimension_semantics=("parallel", "arbitrary"),
            vmem_limit_bytes=_vmem_limit(blocks, scratch + resident, temps)),
    )(x1, g_pre, wup, cw, cb, wd, g_post)


def _rope_tables(seq_len):
    inv_freq = ROPE_THETA ** (-jnp.arange(0, HEAD_DIM, 2, dtype=F32) / HEAD_DIM)
    ang = jnp.arange(seq_len, dtype=F32)[:, None] * inv_freq[None, :]
    cos, sin = jnp.cos(ang), jnp.sin(ang)
    reps = V7X_LANES // HEAD_DIM
    cos_t = jnp.tile(jnp.concatenate([cos, cos], axis=1), (1, reps))
    sin_t = jnp.tile(jnp.concatenate([-sin, sin], axis=1), (1, reps))
    return cos_t, sin_t


def kernel(x, norm_mix_pre, w_in, w_pool, b_pool, pool_scale, w_out, norm_mix_post,
           norm_ffn_pre, w_up, conv_w, conv_b, w_down, norm_ffn_post):
    B, S, D = x.shape
    assert D == D_MODEL and S % (TILES_PER_STEP * SEQ_TILE) == 0 and D_FF % FF_CHUNK == 0
    assert S % MOBA_QUERY_TILE == 0 and MOBA_QUERY_TILE == 2 * MOBA_BLOCK
    depth = w_in.shape[0]
    cos_t, sin_t = _rope_tables(S)
    for l in range(depth):
        qt, k, vt, yp, w_out_bf, w_up_bf, w_down_bf = _mix_in(
            x, norm_mix_pre[l][None, :], w_in[l], cos_t, sin_t,
            w_pool[l], b_pool[l][None, :], pool_scale[l][None, :],
            (w_out[l], w_up[l], w_down[l]))
        yt = _moba(qt, k, vt)
        x1 = _mix_out(yt, yp, x, w_out_bf, norm_mix_post[l][None, :])
        x = _conv_ffn(
            x1, norm_ffn_pre[l][None, :], w_up_bf,
            conv_w[l].reshape(CONV_WIDTH, 2 * D_FF), conv_b[l][None, :],
            w_down_bf, norm_ffn_post[l][None, :])
    return x
```
